```python
import jax, jax.numpy as jnp
from jax import lax
import numpy as np

D_MODEL = 1024
BATCH = 2
SEQ = 8192
DEPTH = 1
DEC_BATCH = 128
DEC_SEQ = 4
PAST_LEN = 8192
PAGE_SIZE = 128

N_HEADS = 16
HEAD_DIM = 64
ATT_WIDTH = N_HEADS * HEAD_DIM
LRU_WIDTH = 1024
LRU_BLOCKS = 8
LRU_BLOCK = LRU_WIDTH // LRU_BLOCKS
CONV_W = 4
LRU_C = 8.0
D_FF = -(-8 * D_MODEL // (3 * 256)) * 256
Q_BLOCK = 128
EPS = 1e-6
ATT_BIAS_INIT = -7.0
IN_WIDTH = 3 * ATT_WIDTH + LRU_WIDTH + 2 * D_MODEL
SPLITS = (ATT_WIDTH, 2 * ATT_WIDTH, 3 * ATT_WIDTH, 3 * ATT_WIDTH + LRU_WIDTH,
          3 * ATT_WIDTH + LRU_WIDTH + D_MODEL)

kernel_name = "stickbreak_rglru_parallel_gated_step"


def _rmsnorm(x, g):
    xf = x.astype(jnp.float32)
    y = xf * lax.rsqrt(jnp.mean(xf * xf, axis=-1, keepdims=True) + EPS)
    return (y * g.astype(jnp.float32)).astype(x.dtype)


def _sb_weights(z, mask):
    log_beta = jax.nn.log_sigmoid(z)
    log_1mb = jnp.where(mask, jax.nn.log_sigmoid(-z), 0.0)
    rc = lax.cumsum(log_1mb, axis=z.ndim - 1, reverse=True)
    suffix = jnp.concatenate([rc[..., 1:], jnp.zeros_like(rc[..., :1])], axis=-1)
    return jnp.where(mask, jnp.exp(log_beta + suffix), 0.0)


def _sb_attn_prompt(q, k, v, b_att):
    n, t = q.shape[0], q.shape[1]
    nb = t // Q_BLOCK
    qb = q.reshape(n, nb, Q_BLOCK, N_HEADS, HEAD_DIM).transpose(1, 0, 2, 3, 4)
    s_pos = jnp.arange(t)
    bias = b_att.astype(jnp.float32)[None, :, None, None]

    def block(args):
        qi, bi = args
        z = jnp.einsum('bqhd,bshd->bhqs', qi, k).astype(jnp.float32) * (HEAD_DIM ** -0.5) + bias
        t_pos = bi * Q_BLOCK + jnp.arange(Q_BLOCK)
        mask = s_pos[None, :] < t_pos[:, None]
        a = _sb_weights(z, mask)
        return jnp.einsum('bhqs,bshd->bqhd', a.astype(v.dtype), v)

    o = lax.map(block, (qb, jnp.arange(nb)))
    return o.transpose(1, 0, 2, 3, 4).reshape(n, t, ATT_WIDTH)


def _sb_attn_sample(q, k_new, v_new, b_att, k_past, v_past):
    n, tn = q.shape[0], q.shape[1]
    p = k_past.shape[1]
    z_past = jnp.einsum('bqhd,bshd->bhqs', q, k_past)
    z_new = jnp.einsum('bqhd,bshd->bhqs', q, k_new)
    z = (jnp.concatenate([z_past, z_new], axis=-1).astype(jnp.float32) * (HEAD_DIM ** -0.5)
         + b_att.astype(jnp.float32)[None, :, None, None])
    ar = jnp.arange(tn)
    mask = jnp.concatenate([jnp.ones((tn, p), bool), ar[None, :] < ar[:, None]], axis=-1)
    a = _sb_weights(z, mask).astype(v_new.dtype)
    o = (jnp.einsum('bhqs,bshd->bqhd', a[..., :p], v_past)
         + jnp.einsum('bhqs,bshd->bqhd', a[..., p:], v_new))
    return o.reshape(n, tn, ATT_WIDTH)


def _causal_conv(hist, w, b, t):
    y = b
    for tap in range(CONV_W):
        y = y + hist[:, tap:tap + t] * w[tap]
    return y


def _rglru(xc, h0, w_r, b_r, w_i, b_i, lam):
    n, t = xc.shape[0], xc.shape[1]
    xb = xc.reshape(n, t, LRU_BLOCKS, LRU_BLOCK)
    r = jax.nn.sigmoid(jnp.einsum('btnc,ncd->btnd', xb, w_r).reshape(n, t, LRU_WIDTH) + b_r)
    i = jax.nn.sigmoid(jnp.einsum('btnc,ncd->btnd', xb, w_i).reshape(n, t, LRU_WIDTH) + b_i)
    log_a = -LRU_C * r.astype(jnp.float32) * jax.nn.softplus(-lam.astype(jnp.float32))
    a = jnp.exp(log_a)
    u = jnp.sqrt(-jnp.expm1(2.0 * log_a)) * (i * xc).astype(jnp.float32)

    def step(h, au):
        a_t, u_t = au
        h = a_t * h + u_t
        return h, h

    h_fin, hs = lax.scan(step, h0.astype(jnp.float32),
                         (a.transpose(1, 0, 2), u.transpose(1, 0, 2)))
    return hs.transpose(1, 0, 2).astype(xc.dtype), h_fin.astype(h0.dtype)


def _layer(x, att_fn, conv_hist, h0, g_mix, w_in, b_att, w_conv, b_conv, w_r, b_r, w_i, b_i, lam,
           w_branch_a, w_branch_b, w_o, g_ffn, w_gate, w_up, w_down):
    n, t = x.shape[0], x.shape[1]
    h = _rmsnorm(x, g_mix)
    q, k, v, xl, ga, gb = jnp.split(h @ w_in, SPLITS, axis=-1)
    q = q.reshape(n, t, N_HEADS, HEAD_DIM)
    k = k.reshape(n, t, N_HEADS, HEAD_DIM)
    v = v.reshape(n, t, N_HEADS, HEAD_DIM)
    o_att = att_fn(q, k, v, b_att)
    hist = jnp.concatenate([conv_hist.astype(xl.dtype), xl], axis=1)
    xc = _causal_conv(hist, w_conv, b_conv, t)
    y_lru, h_fin = _rglru(xc, h0, w_r, b_r, w_i, b_i, lam)
    mixed = jax.nn.sigmoid(ga) * (o_att @ w_branch_a) + jax.nn.sigmoid(gb) * (y_lru @ w_branch_b)
    x = x + mixed @ w_o
    h2 = _rmsnorm(x, g_ffn)
    x = x + (jax.nn.silu(h2 @ w_gate) * (h2 @ w_up)) @ w_down
    return x, k, v, hist[:, -(CONV_W - 1):], h_fin


def setup_inputs(seed: int = 0) -> dict:
    key = jax.random.key(seed)
    ks = jax.random.split(key, 28)
    n_pages = PAST_LEN // PAGE_SIZE
    n_used = DEC_BATCH * n_pages
    n_pool = n_used + max(1, n_used // 4)
    f32 = jnp.float32

    def nrm(k, shape, scale):
        return jax.random.normal(k, shape, f32) * scale

    a0 = jax.random.uniform(ks[10], (DEPTH, LRU_WIDTH), f32, 0.9, 0.999)
    page_table = jax.random.permutation(ks[6], n_pool)[:n_used].reshape(DEC_BATCH, n_pages).astype(jnp.int32)
    return {
        "x_prompt": nrm(ks[0], (BATCH, SEQ, D_MODEL), 1.0),
        "x_sample": nrm(ks[1], (DEC_BATCH, DEC_SEQ, D_MODEL), 1.0),
        "cache_k": nrm(ks[2], (DEPTH, n_pool, PAGE_SIZE, N_HEADS, HEAD_DIM), 1.0),
        "cache_v": nrm(ks[3], (DEPTH, n_pool, PAGE_SIZE, N_HEADS, HEAD_DIM), 1.0),
        "state_conv": nrm(ks[4], (DEPTH, DEC_BATCH, CONV_W - 1, LRU_WIDTH), 1.0),
        "state_lru": nrm(ks[5], (DEPTH, DEC_BATCH, LRU_WIDTH), 0.5),
        "page_table": page_table,
        "g_mix_norm": 1.0 + nrm(ks[7], (DEPTH, D_MODEL), 0.02),
        "w_in": nrm(ks[8], (DEPTH, D_MODEL, IN_WIDTH), D_MODEL ** -0.5),
        "b_att": ATT_BIAS_INIT + nrm(ks[24], (DEPTH, N_HEADS), 0.5),
        "w_conv": nrm(ks[9], (DEPTH, CONV_W, LRU_WIDTH), CONV_W ** -0.5),
        "b_conv": nrm(ks[11], (DEPTH, LRU_WIDTH), 0.01),
        "w_r": nrm(ks[12], (DEPTH, LRU_BLOCKS, LRU_BLOCK, LRU_BLOCK), LRU_BLOCK ** -0.5),
        "b_r": nrm(ks[13], (DEPTH, LRU_WIDTH), 0.01),
        "w_i": nrm(ks[14], (DEPTH, LRU_BLOCKS, LRU_BLOCK, LRU_BLOCK), LRU_BLOCK ** -0.5),
        "b_i": nrm(ks[15], (DEPTH, LRU_WIDTH), 0.01),
        "lam": jnp.log(a0) - jnp.log1p(-a0),
        "w_branch_a": nrm(ks[16], (DEPTH, ATT_WIDTH, D_MODEL), ATT_WIDTH ** -0.5),
        "w_branch_b": nrm(ks[17], (DEPTH, LRU_WIDTH, D_MODEL), LRU_WIDTH ** -0.5),
        "w_o": nrm(ks[18], (DEPTH, D_MODEL, D_MODEL), D_MODEL ** -0.5),
        "g_ffn_norm": 1.0 + nrm(ks[19], (DEPTH, D_MODEL), 0.02),
        "w_gate": nrm(ks[20], (DEPTH, D_MODEL, D_FF), D_MODEL ** -0.5),
        "w_up": nrm(ks[21], (DEPTH, D_MODEL, D_FF), D_MODEL ** -0.5),
        "w_down": nrm(ks[22], (DEPTH, D_FF, D_MODEL), D_FF ** -0.5),
        "g_final": 1.0 + nrm(ks[23], (D_MODEL,), 0.02),
    }


def reference(x_prompt, x_sample, cache_k, cache_v, state_conv, state_lru, page_table,
              g_mix_norm, w_in, b_att, w_conv, b_conv, w_r, b_r, w_i, b_i, lam,
              w_branch_a, w_branch_b, w_o, g_ffn_norm, w_gate, w_up, w_down, g_final):
    n_prompt = x_prompt.shape[0]
    n_dec = x_sample.shape[0]
    past = page_table.shape[1] * PAGE_SIZE
    xp, xs = x_prompt, x_sample
    kp_l, vp_l, cp_l, hp_l, ks_l, vs_l, cs_l, hs_l = [], [], [], [], [], [], [], []
    for l in range(DEPTH):
        lw = (g_mix_norm[l], w_in[l], b_att[l], w_conv[l], b_conv[l], w_r[l], b_r[l], w_i[l], b_i[l],
              lam[l], w_branch_a[l], w_branch_b[l], w_o[l], g_ffn_norm[l], w_gate[l], w_up[l], w_down[l])
        conv0 = jnp.zeros((n_prompt, CONV_W - 1, LRU_WIDTH), xp.dtype)
        h0 = jnp.zeros((n_prompt, LRU_WIDTH), state_lru.dtype)
        xp, kp, vp, cp, hp = _layer(xp, _sb_attn_prompt, conv0, h0, *lw)
        k_past = cache_k[l][page_table].reshape(n_dec, past, N_HEADS, HEAD_DIM)
        v_past = cache_v[l][page_table].reshape(n_dec, past, N_HEADS, HEAD_DIM)

        def att_sample(q, k, v, b, k_past=k_past, v_past=v_past):
            return _sb_attn_sample(q, k, v, b, k_past, v_past)

        xs, ksn, vsn, csn, hsn = _layer(xs, att_sample, state_conv[l], state_lru[l], *lw)
        kp_l.append(kp); vp_l.append(vp); cp_l.append(cp); hp_l.append(hp)
        ks_l.append(ksn); vs_l.append(vsn); cs_l.append(csn); hs_l.append(hsn)
    y_prompt = _rmsnorm(xp, g_final)
    y_sample = _rmsnorm(xs, g_final)
    return (y_prompt, y_sample,
            jnp.stack(kp_l), jnp.stack(vp_l), jnp.stack(cp_l), jnp.stack(hp_l),
            jnp.stack(ks_l), jnp.stack(vs_l), jnp.stack(cs_l), jnp.stack(hs_l))
```

```python
import functools

import jax
import jax.numpy as jnp
from jax import lax
from jax.experimental import pallas as pl
from jax.experimental.pallas import tpu as pltpu

F32 = jnp.float32
BF16 = jnp.bfloat16

LANES = 128
SUBLANES = 8
VMEM_LIMIT_BYTES = 56 * 1024 * 1024

HEAD_DIM = 64
N_HEADS = 16
LRU_BLOCK = 128
CONV_W = 4
LRU_C = 8.0
EPS = 1e-6
LOG2E = 1.4426950408889634
LN2 = 0.6931471805599453

NT_DIMS = (((1,), (1,)), ((), ()))
TN_DIMS = (((0,), (0,)), ((), ()))


def _params(*sem):
    return pltpu.CompilerParams(dimension_semantics=sem, vmem_limit_bytes=VMEM_LIMIT_BYTES)


def _rmsnorm(x, g):
    ms = jnp.mean(x * x, axis=-1, keepdims=True)
    return x * lax.rsqrt(ms + EPS) * g


def _sp2(z):
    return jnp.maximum(z, 0.0) + jnp.log(1.0 + jnp.exp2(-jnp.abs(z))) * LOG2E


def _in_proj_body(x_ref, g_ref, w_ref, k_ref, v_ref, qs_ref, kb_ref, vb_ref, xl_ref,
                  ga_ref, gb_ref, h_scr):
    j = pl.program_id(1)

    @pl.when(j == 0)
    def _():
        h_scr[...] = _rmsnorm(x_ref[...], g_ref[...]).astype(BF16)

    y = jnp.dot(h_scr[...], w_ref[...], preferred_element_type=F32)

    @pl.when(j == 0)
    def _():
        qs_ref[...] = (y * (HEAD_DIM ** -0.5 * LOG2E)).astype(BF16)

    @pl.when(j == 1)
    def _():
        k_ref[...] = y
        kb_ref[...] = y.astype(BF16)

    @pl.when(j == 2)
    def _():
        v_ref[...] = y
        vb_ref[...] = y.astype(BF16)

    @pl.when(j == 3)
    def _():
        xl_ref[...] = y

    @pl.when(j == 4)
    def _():
        ga_ref[...] = jax.nn.sigmoid(y)

    @pl.when(j == 5)
    def _():
        gb_ref[...] = jax.nn.sigmoid(y)


def _in_proj(x, g, w_bf, tm):
    m, d = x.shape
    n_groups = w_bf.shape[1] // d
    out_spec = pl.BlockSpec((tm, d), lambda i, j: (i, 0))
    f32o = jax.ShapeDtypeStruct((m, d), F32)
    bf16o = jax.ShapeDtypeStruct((m, d), BF16)
    return pl.pallas_call(
        _in_proj_body,
        grid=(m // tm, n_groups),
        in_specs=[pl.BlockSpec((tm, d), lambda i, j: (i, 0)),
                  pl.BlockSpec((1, d), lambda i, j: (0, 0)),
                  pl.BlockSpec((d, d), lambda i, j: (0, j))],
        out_specs=[out_spec] * 8,
        out_shape=[f32o, f32o, bf16o, bf16o, bf16o, f32o, f32o, f32o],
        scratch_shapes=[pltpu.VMEM((tm, d), BF16)],
        compiler_params=_params("arbitrary", "arbitrary"),
        name="in_proj",
    )(x, g, w_bf)


def _attn_prompt_body(b_ref, q_ref, k_ref, v_ref, tri_ref, o_ref, acc_ref, c_ref, *, tq, tk):
    p = pl.program_id(1)
    i = pl.program_id(2)
    q = q_ref[...]
    lane = lax.broadcasted_iota(jnp.int32, (tq, LANES), 1)
    zero = jnp.zeros_like(q)
    qm = (jnp.where(lane < HEAD_DIM, q, zero), jnp.where(lane >= HEAD_DIM, q, zero))
    bias = (b_ref[2 * p], b_ref[2 * p + 1])
    acc_ref[...] = jnp.zeros_like(acc_ref)
    c_ref[...] = jnp.zeros_like(c_ref)
    row = lax.broadcasted_iota(jnp.int32, (tq, tk), 0)
    col = lax.broadcasted_iota(jnp.int32, (tq, tk), 1)
    diag_mask = col < row

    def block(j, mask):
        off = pl.multiple_of(j * tk, tk)
        kj = k_ref[pl.ds(off, tk), :]
        vj = v_ref[pl.ds(off, tk), :]
        for hh in range(2):
            z = lax.dot_general(qm[hh], kj, NT_DIMS, preferred_element_type=F32) + bias[hh]
            sp = _sp2(z)
            if mask is not None:
                sp = jnp.where(mask, sp, 0.0)
            cum = jnp.dot(sp.astype(BF16), tri_ref[...], preferred_element_type=F32)
            c = c_ref[hh]
            a = jnp.exp2(z + cum + jnp.concatenate([c] * (tk // LANES), axis=1))
            if mask is not None:
                a = jnp.where(mask, a, 0.0)
            acc_ref[hh] += jnp.dot(a.astype(BF16), vj, preferred_element_type=F32)
            c_ref[hh] = c + jnp.broadcast_to(cum[:, :1], (tq, LANES))

    block(i, diag_mask)

    def body(jj, carry):
        block(i - 1 - jj, None)
        return carry

    lax.fori_loop(0, i, body, 0)
    o_ref[...] = jnp.where(lane < HEAD_DIM, acc_ref[0], acc_ref[1]).astype(o_ref.dtype)


def _attn_prompt(qs, kb, vb, bias2, n_seq, seq, tq):
    tk = tq
    width = qs.shape[1]
    n_pairs = width // LANES
    nq = seq // tq
    r = lax.broadcasted_iota(jnp.int32, (tk, tk), 0)
    c = lax.broadcasted_iota(jnp.int32, (tk, tk), 1)
    tri = jnp.where(r >= c, -1.0, 0.0).astype(BF16)
    return pl.pallas_call(
        functools.partial(_attn_prompt_body, tq=tq, tk=tk),
        grid=(n_seq, n_pairs, nq),
        in_specs=[pl.BlockSpec(memory_space=pltpu.SMEM),
                  pl.BlockSpec((tq, LANES), lambda b, p, i: (b * nq + i, p)),
                  pl.BlockSpec((seq, LANES), lambda b, p, i: (b, p)),
                  pl.BlockSpec((seq, LANES), lambda b, p, i: (b, p)),
                  pl.BlockSpec((tk, tk), lambda b, p, i: (0, 0))],
        out_specs=pl.BlockSpec((tq, LANES), lambda b, p, i: (b * nq + i, p)),
        out_shape=jax.ShapeDtypeStruct(qs.shape, BF16),
        scratch_shapes=[pltpu.VMEM((2, tq, LANES), F32), pltpu.VMEM((2, tq, LANES), F32)],
        compiler_params=_params("arbitrary", "arbitrary", "arbitrary"),
        name="attn_prompt",
    )(bias2, qs, kb, vb, tri)


def _attn_sample_body(pt_ref, q_ref, knt_ref, vnt_ref, bias_ref, tri_ref, sel_ref, *rest,
                      n_pages_step, n_new):
    del pt_ref
    kt_refs = rest[:n_pages_step]
    vt_refs = rest[n_pages_step:2 * n_pages_step]
    o_ref, qbd_ref, acc_ref, c_ref = rest[2 * n_pages_step:]
    jj = pl.program_id(1)
    n_cols, width = qbd_ref.shape
    page = tri_ref.shape[0]

    def keys_block(kt_list, vt_list, mask):
        n = len(kt_list)
        cat = lambda xs: xs[0] if len(xs) == 1 else jnp.concatenate(xs, axis=1)
        z = jnp.dot(qbd_ref[...], cat(kt_list), preferred_element_type=F32) + cat([bias_ref[...]] * n)
        sp = _sp2(z)
        if mask is not None:
            sp = jnp.where(mask, sp, 0.0)
        c = c_ref[...]
        a_chunks = [None] * n
        for ch in reversed(range(n)):
            sl = slice(ch * page, (ch + 1) * page)
            cum = jnp.dot(sp[:, sl].astype(BF16), tri_ref[...], preferred_element_type=F32)
            a = jnp.exp2(z[:, sl] + cum + c)
            if mask is not None:
                a = jnp.where(mask, a, 0.0)
            a_chunks[ch] = a
            c = c + jnp.broadcast_to(cum[:, :1], c.shape)
        c_ref[...] = c
        acc_ref[...] += lax.dot_general(cat(vt_list), cat(a_chunks), NT_DIMS,
                                        preferred_element_type=F32)

    @pl.when(jj == 0)
    def _():
        r = lax.broadcasted_iota(jnp.int32, (n_cols, width), 0)
        l = lax.broadcasted_iota(jnp.int32, (n_cols, width), 1)
        on_head = (l // HEAD_DIM == r // n_new) & (r < N_HEADS * n_new)
        q = q_ref[...].astype(F32)
        qbd = jnp.zeros((n_cols, width), F32)
        for t in range(n_new):
            qbd = jnp.where(on_head & (r % n_new == t), q[t:t + 1, :], qbd)
        qbd_ref[...] = qbd
        acc_ref[...] = jnp.zeros_like(acc_ref)
        c_ref[...] = jnp.zeros_like(c_ref)
        col = lax.broadcasted_iota(jnp.int32, (n_cols, page), 0)
        key = lax.broadcasted_iota(jnp.int32, (n_cols, page), 1)
        keys_block([knt_ref[...]], [vnt_ref[...]], key < col % n_new)

    keys_block([ref[...] for ref in kt_refs], [ref[...] for ref in vt_refs], None)

    @pl.when(jj == pl.num_programs(1) - 1)
    def _():
        ch = lax.broadcasted_iota(jnp.int32, (width, n_cols), 0)
        col = lax.broadcasted_iota(jnp.int32, (width, n_cols), 1)
        own = jnp.where(ch // HEAD_DIM == col // n_new, acc_ref[...], 0.0)
        o_ref[...] = lax.dot_general(sel_ref[...], own.astype(BF16), NT_DIMS,
                                     preferred_element_type=F32)


def _attn_sample(q_bm, knt, vnt, bias_rep, cache_kt, cache_vt, page_table, n_pages_step):
    n_dec, n_new, width = q_bm.shape
    n_pages = page_table.shape[1]
    page = cache_kt.shape[2]
    n_steps = n_pages // n_pages_step
    n_cols = LANES
    rr = lax.broadcasted_iota(jnp.int32, (page, page), 0)
    cc = lax.broadcasted_iota(jnp.int32, (page, page), 1)
    tri = jnp.where(rr >= cc, -1.0, 0.0).astype(BF16)
    sr = lax.broadcasted_iota(jnp.int32, (SUBLANES, n_cols), 0)
    sc = lax.broadcasted_iota(jnp.int32, (SUBLANES, n_cols), 1)
    sel = jnp.where((sc % n_new == sr) & (sc < N_HEADS * n_new) & (sr < n_new), 1.0, 0.0).astype(BF16)

    def page_map(pg):
        def index_map(b, jj, pt):
            logical = (n_steps - 1 - jj) * n_pages_step + pg
            return (pt[b * n_pages + logical], 0, 0)
        return index_map

    const = lambda shape: pl.BlockSpec(shape, lambda b, jj, pt: (0,) * len(shape))
    page_spec = lambda index_map: pl.BlockSpec((None, width, page), index_map)
    new_spec = page_spec(lambda b, jj, pt: (b, 0, 0))
    page_specs = [page_spec(page_map(pg)) for pg in range(n_pages_step)]
    grid_spec = pltpu.PrefetchScalarGridSpec(
        num_scalar_prefetch=1,
        grid=(n_dec, n_steps),
        in_specs=[pl.BlockSpec((None, n_new, width), lambda b, jj, pt: (b, 0, 0)),
                  new_spec, new_spec,
                  const((n_cols, page)), const((page, page)), const((SUBLANES, n_cols))]
                 + page_specs + page_specs,
        out_specs=pl.BlockSpec((None, SUBLANES, width), lambda b, jj, pt: (b, 0, 0)),
        scratch_shapes=[pltpu.VMEM((n_cols, width), F32),
                        pltpu.VMEM((width, n_cols), F32),
                        pltpu.VMEM((n_cols, page), F32)],
    )
    return pl.pallas_call(
        functools.partial(_attn_sample_body, n_pages_step=n_pages_step, n_new=n_new),
        grid_spec=grid_spec,
        out_shape=jax.ShapeDtypeStruct((n_dec, SUBLANES, width), F32),
        compiler_params=_params("arbitrary", "arbitrary"),
        name="attn_sample",
    )(page_table.reshape(-1), q_bm, knt, vnt, bias_rep, tri, sel,
      *([cache_kt] * n_pages_step), *([cache_vt] * n_pages_step))


def _lru_gates(xc, wr_ref, br_ref, wi_ref, bi_ref, lam_ref):
    lam = lam_ref[...]
    neg_c_softplus = -LRU_C * (jnp.maximum(-lam, 0.0) + jnp.log1p(jnp.exp(-jnp.abs(lam))))
    xcb = xc.astype(BF16)
    a_parts, u_parts = [], []
    for n in range(xc.shape[1] // LRU_BLOCK):
        sl = slice(n * LRU_BLOCK, (n + 1) * LRU_BLOCK)
        r = jax.nn.sigmoid(jnp.dot(xcb[:, sl], wr_ref[n], preferred_element_type=F32) + br_ref[:, sl])
        ig = jax.nn.sigmoid(jnp.dot(xcb[:, sl], wi_ref[n], preferred_element_type=F32) + bi_ref[:, sl])
        a = jnp.exp(r * neg_c_softplus[:, sl])
        a_parts.append(a)
        u_parts.append(jnp.sqrt(1.0 - a * a) * (ig * xc[:, sl]))
    return a_parts, u_parts


def _lru_prompt_body(xl_ref, wc_ref, bc_ref, wr_ref, br_ref, wi_ref, bi_ref, lam_ref,
                     y_ref, hfin_ref, xbuf, h_scr, a_scr, u_scr, *, tb):
    t = pl.program_id(1)
    width = xl_ref.shape[1]
    pad = SUBLANES

    @pl.when(t == 0)
    def _():
        xbuf[0:pad, :] = jnp.zeros((pad, width), F32)
        h_scr[...] = jnp.zeros_like(h_scr)

    x = xl_ref[...]
    xbuf[pad:pad + tb, :] = x
    xc = bc_ref[...]
    for tap in range(CONV_W - 1):
        xc = xc + xbuf[pl.ds(pad - (CONV_W - 1) + tap, tb), :] * wc_ref[tap:tap + 1, :]
    xc = xc + x * wc_ref[CONV_W - 1:CONV_W, :]
    xbuf[0:pad, :] = x[tb - pad:, :]

    a_parts, u_parts = _lru_gates(xc, wr_ref, br_ref, wi_ref, bi_ref, lam_ref)
    for n, (a, u) in enumerate(zip(a_parts, u_parts)):
        sl = slice(n * LRU_BLOCK, (n + 1) * LRU_BLOCK)
        a_scr[:, sl] = a
        u_scr[:, sl] = u

    row = lax.broadcasted_iota(jnp.int32, (SUBLANES, width), 0)

    def group(g, h):
        r0 = pl.multiple_of(g * SUBLANES, SUBLANES)
        a = a_scr[pl.ds(r0, SUBLANES), :]
        u = u_scr[pl.ds(r0, SUBLANES), :]
        for d in (1, 2, 4):
            keep = row >= d
            a_prev = jnp.where(keep, pltpu.roll(a, d, 0), 1.0)
            u_prev = jnp.where(keep, pltpu.roll(u, d, 0), 0.0)
            u = u + a * u_prev
            a = a * a_prev
        hs = a * h + u
        u_scr[pl.ds(r0, SUBLANES), :] = hs
        return jnp.broadcast_to(hs[SUBLANES - 1:SUBLANES, :], (SUBLANES, width))

    h = lax.fori_loop(0, tb // SUBLANES, group, h_scr[...])
    h_scr[...] = h
    y_ref[...] = u_scr[...].astype(y_ref.dtype)
    hfin_ref[...] = h[:1, :]


def _lru_prompt(xl, wc, bc, wr_bf, br, wi_bf, bi, lam, n_seq, seq, tb):
    width = xl.shape[1]
    nt = seq // tb
    row_spec = pl.BlockSpec((1, width), lambda b, t: (0, 0))
    w_spec = pl.BlockSpec(wr_bf.shape, lambda b, t: (0, 0, 0))
    return pl.pallas_call(
        functools.partial(_lru_prompt_body, tb=tb),
        grid=(n_seq, nt),
        in_specs=[pl.BlockSpec((tb, width), lambda b, t: (b * nt + t, 0)),
                  pl.BlockSpec((CONV_W, width), lambda b, t: (0, 0)),
                  row_spec, w_spec, row_spec, w_spec, row_spec, row_spec],
        out_specs=[pl.BlockSpec((tb, width), lambda b, t: (b * nt + t, 0)),
                   pl.BlockSpec((None, 1, width), lambda b, t: (b, 0, 0))],
        out_shape=[jax.ShapeDtypeStruct(xl.shape, BF16),
                   jax.ShapeDtypeStruct((n_seq, 1, width), F32)],
        scratch_shapes=[pltpu.VMEM((tb + SUBLANES, width), F32),
                        pltpu.VMEM((SUBLANES, width), F32),
                        pltpu.VMEM((tb, width), F32),
                        pltpu.VMEM((tb, width), F32)],
        compiler_params=_params("arbitrary", "arbitrary"),
        name="lru_prompt",
    )(xl, wc, bc, wr_bf, br, wi_bf, bi, lam)


def _lru_sample_body(xl_ref, c0_ref, h0_ref, wc_ref, bc_ref, wr_ref, br_ref, wi_ref, bi_ref,
                     lam_ref, y_ref, hfin_ref):
    n_new = xl_ref.shape[0]
    hist = [c0_ref[i] for i in range(CONV_W - 1)] + [xl_ref[i] for i in range(n_new)]
    h = h0_ref[...]
    for t in range(n_new):
        xc = bc_ref[...]
        for tap in range(CONV_W):
            xc = xc + hist[tap + t] * wc_ref[tap:tap + 1, :]
        a_parts, u_parts = _lru_gates(xc, wr_ref, br_ref, wi_ref, bi_ref, lam_ref)
        h = jnp.concatenate(a_parts, axis=1) * h + jnp.concatenate(u_parts, axis=1)
        y_ref[t] = h.astype(y_ref.dtype)
    hfin_ref[...] = h


def _lru_sample(xl_tm, conv_tm, h0, wc, bc, wr_bf, br, wi_bf, bi, lam):
    n_new, n_dec, width = xl_tm.shape
    return pl.pallas_call(
        _lru_sample_body,
        out_shape=[jax.ShapeDtypeStruct((n_new, n_dec, width), BF16),
                   jax.ShapeDtypeStruct((n_dec, width), F32)],
        compiler_params=pltpu.CompilerParams(vmem_limit_bytes=VMEM_LIMIT_BYTES),
        name="lru_sample",
    )(xl_tm, conv_tm, h0, wc, bc, wr_bf, br, wi_bf, bi, lam)


def _merge_body(x_ref, oa_ref, yl_ref, ga_ref, gb_ref, wa_ref, wb_ref, wo_ref, g_ref,
                x1_ref, h2_ref):
    pa = jnp.dot(oa_ref[...], wa_ref[...], preferred_element_type=F32)
    pb = jnp.dot(yl_ref[...], wb_ref[...], preferred_element_type=F32)
    mixed = ga_ref[...] * pa + gb_ref[...] * pb
    x1 = x_ref[...] + jnp.dot(mixed.astype(BF16), wo_ref[...], preferred_element_type=F32)
    x1_ref[...] = x1
    h2_ref[...] = _rmsnorm(x1, g_ref[...]).astype(BF16)


def _merge(x, o_att, y_lru, ga, gb, wa_bf, wb_bf, wo_bf, g_ffn, tm):
    m, d = x.shape
    tile = pl.BlockSpec((tm, d), lambda i: (i, 0))
    w_spec = pl.BlockSpec((d, d), lambda i: (0, 0))
    return pl.pallas_call(
        _merge_body,
        grid=(m // tm,),
        in_specs=[tile, tile, tile, tile, tile, w_spec, w_spec, w_spec,
                  pl.BlockSpec((1, d), lambda i: (0, 0))],
        out_specs=[tile, tile],
        out_shape=[jax.ShapeDtypeStruct((m, d), F32), jax.ShapeDtypeStruct((m, d), BF16)],
        compiler_params=_params("arbitrary"),
        name="merge",
    )(x, o_att, y_lru, ga, gb, wa_bf, wb_bf, wo_bf, g_ffn)


def _ffn_body(x1_ref, h2_ref, wg_ref, wu_ref, wd_ref, g_ref, y_ref, acc_ref, *, final_norm):
    f = pl.program_id(1)

    @pl.when(f == 0)
    def _():
        acc_ref[...] = jnp.zeros_like(acc_ref)

    h2 = h2_ref[...]
    gate = jnp.dot(h2, wg_ref[...], preferred_element_type=F32)
    up = jnp.dot(h2, wu_ref[...], preferred_element_type=F32)
    act = (jax.nn.silu(gate) * up).astype(BF16)
    acc_ref[...] += jnp.dot(act, wd_ref[...], preferred_element_type=F32)

    @pl.when(f == pl.num_programs(1) - 1)
    def _():
        x2 = x1_ref[...] + acc_ref[...]
        y_ref[...] = _rmsnorm(x2, g_ref[...]) if final_norm else x2


def _ffn(x1, h2, wg_bf, wu_bf, wd_bf, g_final, final_norm, tm, tf):
    m, d = x1.shape
    d_ff = wg_bf.shape[1]
    tile = pl.BlockSpec((tm, d), lambda i, f: (i, 0))
    return pl.pallas_call(
        functools.partial(_ffn_body, final_norm=final_norm),
        grid=(m // tm, d_ff // tf),
        in_specs=[tile, tile,
                  pl.BlockSpec((d, tf), lambda i, f: (0, f)),
                  pl.BlockSpec((d, tf), lambda i, f: (0, f)),
                  pl.BlockSpec((tf, d), lambda i, f: (f, 0)),
                  pl.BlockSpec((1, d), lambda i, f: (0, 0))],
        out_specs=tile,
        out_shape=jax.ShapeDtypeStruct((m, d), F32),
        scratch_shapes=[pltpu.VMEM((tm, d), F32)],
        compiler_params=_params("arbitrary", "arbitrary"),
        name="ffn",
    )(x1, h2, wg_bf, wu_bf, wd_bf, g_final)


TOKEN_TILE = 512
FFN_TILE = 256
ATTN_TILE = 256
LRU_TIME_TILE = 512
PAGES_PER_STEP = 4


def kernel(x_prompt, x_sample, cache_k, cache_v, state_conv, state_lru, page_table, g_mix_norm, w_in, b_att, w_conv, b_conv, w_r, b_r, w_i, b_i, lam, w_branch_a, w_branch_b, w_o, g_ffn_norm, w_gate, w_up, w_down, g_final):
    n_prompt, seq, d = x_prompt.shape
    n_dec, n_new, _ = x_sample.shape
    depth = w_in.shape[0]
    n_pool, page = cache_k.shape[1], cache_k.shape[2]
    width = N_HEADS * HEAD_DIM
    lru_width = w_conv.shape[2]
    assert d == width == lru_width, "kernels assume one shared channel width"

    xp = x_prompt.reshape(n_prompt * seq, d)
    xs = x_sample.transpose(1, 0, 2).reshape(n_new * n_dec, d)
    row = lambda v: v.reshape(1, -1)
    outs = [[] for _ in range(8)]
    for l in range(depth):
        last = l == depth - 1
        w_in_bf = w_in[l].astype(BF16)
        wr_bf, wi_bf = w_r[l].astype(BF16), w_i[l].astype(BF16)
        wa_bf, wb_bf, wo_bf = (w_branch_a[l].astype(BF16), w_branch_b[l].astype(BF16),
                               w_o[l].astype(BF16))
        wg_bf, wu_bf, wd_bf = w_gate[l].astype(BF16), w_up[l].astype(BF16), w_down[l].astype(BF16)
        bias2 = b_att[l].astype(F32) * LOG2E
        lru_w = (w_conv[l], row(b_conv[l]), wr_bf, row(b_r[l]), wi_bf, row(b_i[l]), row(lam[l]))
        g_next = row(g_final)

        k_p, v_p, qs, kb, vb, xl, ga, gb = _in_proj(xp, row(g_mix_norm[l]), w_in_bf, TOKEN_TILE)
        o_att = _attn_prompt(qs, kb, vb, bias2, n_prompt, seq, ATTN_TILE)
        y_lru, h_p = _lru_prompt(xl, *lru_w, n_prompt, seq, LRU_TIME_TILE)
        x1, h2 = _merge(xp, o_att, y_lru, ga, gb, wa_bf, wb_bf, wo_bf, row(g_ffn_norm[l]), TOKEN_TILE)
        xp = _ffn(x1, h2, wg_bf, wu_bf, wd_bf, g_next, last, TOKEN_TILE, FFN_TILE)
        outs[0].append(k_p.reshape(n_prompt, seq, N_HEADS, HEAD_DIM))
        outs[1].append(v_p.reshape(n_prompt, seq, N_HEADS, HEAD_DIM))
        outs[2].append(xl.reshape(n_prompt, seq, lru_width)[:, seq - (CONV_W - 1):])
        outs[3].append(h_p.reshape(n_prompt, lru_width))

        k_s, v_s, qs, _, _, xl, ga, gb = _in_proj(xs, row(g_mix_norm[l]), w_in_bf, n_new * n_dec)
        to_bm = lambda v: v.reshape(n_new, n_dec, -1).transpose(1, 0, 2)
        k_bm, v_bm = to_bm(k_s), to_bm(v_s)
        bias_cols = jnp.pad(jnp.repeat(bias2, n_new), (0, LANES - N_HEADS * n_new))
        bias_rep = jnp.broadcast_to(bias_cols[:, None], (LANES, page))
        to_pages_t = lambda c: c.transpose(0, 2, 3, 1).reshape(c.shape[0], width, page)
        new_page_t = lambda v: jnp.pad(v.transpose(0, 2, 1), ((0, 0), (0, 0), (0, page - n_new)))
        o8 = _attn_sample(to_bm(qs), new_page_t(k_bm), new_page_t(v_bm), bias_rep,
                          to_pages_t(cache_k[l]), to_pages_t(cache_v[l]), page_table, PAGES_PER_STEP)
        o_att = o8[:, :n_new].transpose(1, 0, 2).reshape(n_new * n_dec, width).astype(BF16)
        xl_tm = xl.reshape(n_new, n_dec, lru_width)
        conv_tm = state_conv[l].transpose(1, 0, 2)
        y_lru, h_s = _lru_sample(xl_tm, conv_tm, state_lru[l], *lru_w)
        x1, h2 = _merge(xs, o_att, y_lru.reshape(n_new * n_dec, lru_width), ga, gb,
                        wa_bf, wb_bf, wo_bf, row(g_ffn_norm[l]), n_new * n_dec)
        xs = _ffn(x1, h2, wg_bf, wu_bf, wd_bf, g_next, last, n_new * n_dec, FFN_TILE)
        hist_tm = jnp.concatenate([conv_tm, xl_tm], axis=0)
        outs[4].append(k_bm.reshape(n_dec, n_new, N_HEADS, HEAD_DIM))
        outs[5].append(v_bm.reshape(n_dec, n_new, N_HEADS, HEAD_DIM))
        outs[6].append(hist_tm[n_new:n_new + CONV_W - 1].transpose(1, 0, 2))
        outs[7].append(h_s)

    y_prompt = xp.reshape(n_prompt, seq, d)
    y_sample = xs.reshape(n_new, n_dec, d).transpose(1, 0, 2)
    return (y_prompt, y_sample) + tuple(jnp.stack(o) for o in outs)
```

```python
import functools

import jax
import jax.numpy as jnp
from jax import lax
from jax.experimental import pallas as pl
from jax.experimental.pallas import tpu as pltpu

F32 = jnp.float32
BF16 = jnp.bfloat16

LANES = 128
SUBLANES = 8
VMEM_LIMIT_BYTES = 56 * 1024 * 1024

HEAD_DIM = 64
N_HEADS = 16
LRU_BLOCK = 128
CONV_W = 4
LRU_C = 8.0
EPS = 1e-6
LOG2E = 1.4426950408889634
LN2 = 0.6931471805599453

NT_DIMS = (((1,), (1,)), ((), ()))
TN_DIMS = (((0,), (0,)), ((), ()))


def _params(*sem):
    return pltpu.CompilerParams(dimension_semantics=sem, vmem_limit_bytes=VMEM_LIMIT_BYTES)


def _rmsnorm(x, g):
    ms = jnp.mean(x * x, axis=-1, keepdims=True)
    return x * lax.rsqrt(ms + EPS) * g


EXP2_ARG_MAX = 126.0


def _sp2(z):
    return jnp.maximum(z, jnp.log(1.0 + jnp.exp2(jnp.minimum(z, EXP2_ARG_MAX))) * LOG2E)


def _in_proj_body(x_ref, g_ref, w_ref, k_ref, v_ref, qs_ref, kb_ref, vb_ref, xl_ref,
                  ga_ref, gb_ref, h_scr):
    j = pl.program_id(1)

    @pl.when(j == 0)
    def _():
        h_scr[...] = _rmsnorm(x_ref[...], g_ref[...]).astype(BF16)

    y = jnp.dot(h_scr[...], w_ref[...], preferred_element_type=F32)

    @pl.when(j == 0)
    def _():
        qs_ref[...] = (y * (HEAD_DIM ** -0.5 * LOG2E)).astype(BF16)

    @pl.when(j == 1)
    def _():
        k_ref[...] = y
        kb_ref[...] = y.astype(BF16)

    @pl.when(j == 2)
    def _():
        v_ref[...] = y
        vb_ref[...] = y.astype(BF16)

    @pl.when(j == 3)
    def _():
        xl_ref[...] = y

    @pl.when(j == 4)
    def _():
        ga_ref[...] = jax.nn.sigmoid(y)

    @pl.when(j == 5)
    def _():
        gb_ref[...] = jax.nn.sigmoid(y)


def _in_proj(x, g, w_bf, tm):
    m, d = x.shape
    n_groups = w_bf.shape[1] // d
    out_spec = pl.BlockSpec((tm, d), lambda i, j: (i, 0))
    f32o = jax.ShapeDtypeStruct((m, d), F32)
    bf16o = jax.ShapeDtypeStruct((m, d), BF16)
    return pl.pallas_call(
        _in_proj_body,
        grid=(m // tm, n_groups),
        in_specs=[pl.BlockSpec((tm, d), lambda i, j: (i, 0)),
                  pl.BlockSpec((1, d), lambda i, j: (0, 0)),
                  pl.BlockSpec((d, d), lambda i, j: (0, j))],
        out_specs=[out_spec] * 8,
        out_shape=[f32o, f32o, bf16o, bf16o, bf16o, f32o, f32o, f32o],
        scratch_shapes=[pltpu.VMEM((tm, d), BF16)],
        compiler_params=_params("arbitrary", "arbitrary"),
        name="in_proj",
    )(x, g, w_bf)


ATTN_STAGES = 5
MASKED_LOGIT = -1e30


def _attn_prompt_body(it_ref, jt_ref, b_ref, q_ref, k_ref, v_ref, tri_ref, o_ref,
                      acc_ref, c_ref, z_ref, sp_ref, cum_ref, a_ref, *, tq, tk, n_items):
    p = pl.program_id(1)
    bias = (b_ref[2 * p], b_ref[2 * p + 1])
    lane = lax.broadcasted_iota(jnp.int32, (tq, LANES), 1)
    depth = ATTN_STAGES - 1
    n_slots = z_ref.shape[0]
    for ref in (acc_ref, c_ref, z_ref, sp_ref, cum_ref, a_ref):
        ref[...] = jnp.zeros_like(ref)

    def body(n, carry):
        i_out, j_out = it_ref[n], jt_ref[n]
        i_w, j_w = it_ref[n + 1], jt_ref[n + 1]
        i_new, j_new = it_ref[n + depth], jt_ref[n + depth]

        vj = v_ref[pl.ds(pl.multiple_of(j_out * tk, tk), tk), :]
        for hh in range(2):
            prev = jnp.where(j_out == i_out, 0.0, acc_ref[hh])
            acc_ref[hh] = prev + jnp.dot(a_ref[hh], vj, preferred_element_type=F32)

        zs = z_ref[(n + n_slots - 3) % n_slots]
        for hh in range(2):
            cum = cum_ref[hh]
            c = jnp.where(j_w == i_w, 0.0, c_ref[hh])
            a = jnp.exp2(zs[hh] + cum + jnp.concatenate([c] * (tk // LANES), axis=1))
            a_ref[hh] = a.astype(BF16)
            c_ref[hh] = c + jnp.broadcast_to(cum[:, :1], (tq, LANES))

        for hh in range(2):
            cum_ref[hh] = jnp.dot(sp_ref[hh], tri_ref[...], preferred_element_type=F32)

        zs = z_ref[(n + n_slots - 1) % n_slots]
        for hh in range(2):
            sp_ref[hh] = _sp2(zs[hh]).astype(BF16)

        q = q_ref[pl.ds(pl.multiple_of(i_new * tq, tq), tq), :]
        kj = k_ref[pl.ds(pl.multiple_of(j_new * tk, tk), tk), :]
        zero = jnp.zeros_like(q)
        slot = n % n_slots
        q_heads = (jnp.where(lane < HEAD_DIM, q, zero), jnp.where(lane >= HEAD_DIM, q, zero))
        for hh in range(2):
            z = lax.dot_general(q_heads[hh], kj, NT_DIMS, preferred_element_type=F32)
            z_ref[slot, hh] = z + bias[hh]

        @pl.when(j_new == i_new)
        def _():
            row = lax.broadcasted_iota(jnp.int32, (tq, tk), 0)
            col = lax.broadcasted_iota(jnp.int32, (tq, tk), 1)
            for hh in range(2):
                z_ref[slot, hh] = jnp.where(col < row, z_ref[slot, hh], MASKED_LOGIT)

        @pl.when(j_out == 0)
        def _():
            out = jnp.where(lane < HEAD_DIM, acc_ref[0], acc_ref[1])
            o_ref[pl.ds(pl.multiple_of(i_out * tq, tq), tq), :] = out.astype(o_ref.dtype)

        return carry

    lax.fori_loop(0, n_items + depth, body, 0)


def _attn_prompt(qs, kb, vb, bias2, n_seq, seq, tq):
    tk = tq
    width = qs.shape[1]
    n_pairs = width // LANES
    nq = seq // tq
    depth = ATTN_STAGES - 1
    items = [(i, j) for i in range(nq) for j in range(i, -1, -1)]
    padded = [(0, 0)] * depth + items + [(nq - 1, 0)] * depth
    it = jnp.asarray([i for i, _ in padded], jnp.int32)
    jt = jnp.asarray([j for _, j in padded], jnp.int32)
    r = lax.broadcasted_iota(jnp.int32, (tk, tk), 0)
    c = lax.broadcasted_iota(jnp.int32, (tk, tk), 1)
    tri = jnp.where(r >= c, -1.0, 0.0).astype(BF16)
    seq_spec = pl.BlockSpec((seq, LANES), lambda b, p, it, jt: (b, p))
    grid_spec = pltpu.PrefetchScalarGridSpec(
        num_scalar_prefetch=2,
        grid=(n_seq, n_pairs),
        in_specs=[pl.BlockSpec(memory_space=pltpu.SMEM), seq_spec, seq_spec, seq_spec,
                  pl.BlockSpec((tk, tk), lambda b, p, it, jt: (0, 0))],
        out_specs=seq_spec,
        scratch_shapes=[pltpu.VMEM((2, tq, LANES), F32), pltpu.VMEM((2, tq, LANES), F32),
                        pltpu.VMEM((depth, 2, tq, tk), F32), pltpu.VMEM((2, tq, tk), BF16),
                        pltpu.VMEM((2, tq, tk), F32), pltpu.VMEM((2, tq, tk), BF16)],
    )
    return pl.pallas_call(
        functools.partial(_attn_prompt_body, tq=tq, tk=tk, n_items=len(items)),
        grid_spec=grid_spec,
        out_shape=jax.ShapeDtypeStruct(qs.shape, BF16),
        compiler_params=_params("arbitrary", "arbitrary"),
        name="attn_prompt",
    )(it, jt, bias2, qs, kb, vb, tri)


def _attn_sample_body(pt_ref, q_ref, knt_ref, vnt_ref, bias_ref, tri_ref, sel_ref, *rest,
                      n_pages_step, n_new):
    del pt_ref
    kt_refs = rest[:n_pages_step]
    vt_refs = rest[n_pages_step:2 * n_pages_step]
    o_ref, qbd_ref, acc_ref, c_ref = rest[2 * n_pages_step:]
    jj = pl.program_id(1)
    n_cols, width = qbd_ref.shape
    page = tri_ref.shape[0]

    def keys_block(kt_list, vt_list, c, mask):
        n = len(kt_list)
        cat = lambda xs: xs[0] if len(xs) == 1 else jnp.concatenate(xs, axis=1)
        z = jnp.dot(qbd_ref[...], cat(kt_list), preferred_element_type=F32) + cat([bias_ref[...]] * n)
        sp = _sp2(z)
        if mask is not None:
            sp = jnp.where(mask, sp, 0.0)
        a_chunks = [None] * n
        for ch in reversed(range(n)):
            sl = slice(ch * page, (ch + 1) * page)
            cum = jnp.dot(sp[:, sl].astype(BF16), tri_ref[...], preferred_element_type=F32)
            a = jnp.exp2(z[:, sl] + cum + c)
            if mask is not None:
                a = jnp.where(mask, a, 0.0)
            a_chunks[ch] = a
            c = c + jnp.broadcast_to(cum[:, :1], c.shape)
        return c, lax.dot_general(cat(vt_list), cat(a_chunks), NT_DIMS, preferred_element_type=F32)

    @pl.when(jj == 0)
    def _():
        r = lax.broadcasted_iota(jnp.int32, (n_cols, width), 0)
        l = lax.broadcasted_iota(jnp.int32, (n_cols, width), 1)
        on_head = (l // HEAD_DIM == r // n_new) & (r < N_HEADS * n_new)
        q = q_ref[...].astype(F32)
        qbd = jnp.zeros((n_cols, width), F32)
        for t in range(n_new):
            qbd = jnp.where(on_head & (r % n_new == t), q[t:t + 1, :], qbd)
        qbd_ref[...] = qbd
        col = lax.broadcasted_iota(jnp.int32, (n_cols, page), 0)
        key = lax.broadcasted_iota(jnp.int32, (n_cols, page), 1)
        c_ref[...], acc_ref[...] = keys_block([knt_ref[...]], [vnt_ref[...]],
                                              jnp.zeros(c_ref.shape, F32), key < col % n_new)

    c = c_ref[...]
    out = None
    for g in reversed(range(n_pages_step // SAMPLE_PAGE_GROUP)):
        refs = slice(g * SAMPLE_PAGE_GROUP, (g + 1) * SAMPLE_PAGE_GROUP)
        c, part = keys_block([ref[...] for ref in kt_refs[refs]], [ref[...] for ref in vt_refs[refs]],
                             c, None)
        out = part if out is None else out + part
    c_ref[...] = c
    acc_ref[...] += out

    @pl.when(jj == pl.num_programs(1) - 1)
    def _():
        ch = lax.broadcasted_iota(jnp.int32, (width, n_cols), 0)
        col = lax.broadcasted_iota(jnp.int32, (width, n_cols), 1)
        own = jnp.where(ch // HEAD_DIM == col // n_new, acc_ref[...], 0.0)
        o_ref[...] = lax.dot_general(sel_ref[...], own.astype(BF16), NT_DIMS,
                                     preferred_element_type=F32)


def _attn_sample(q_bm, knt, vnt, bias_rep, cache_kt, cache_vt, page_table, n_pages_step):
    n_dec, n_new, width = q_bm.shape
    n_pages = page_table.shape[1]
    page = cache_kt.shape[2]
    n_steps = n_pages // n_pages_step
    n_cols = LANES
    rr = lax.broadcasted_iota(jnp.int32, (page, page), 0)
    cc = lax.broadcasted_iota(jnp.int32, (page, page), 1)
    tri = jnp.where(rr >= cc, -1.0, 0.0).astype(BF16)
    sr = lax.broadcasted_iota(jnp.int32, (SUBLANES, n_cols), 0)
    sc = lax.broadcasted_iota(jnp.int32, (SUBLANES, n_cols), 1)
    sel = jnp.where((sc % n_new == sr) & (sc < N_HEADS * n_new) & (sr < n_new), 1.0, 0.0).astype(BF16)

    def page_map(pg):
        def index_map(b, jj, pt):
            logical = (n_steps - 1 - jj) * n_pages_step + pg
            return (pt[b * n_pages + logical], 0, 0)
        return index_map

    const = lambda shape: pl.BlockSpec(shape, lambda b, jj, pt: (0,) * len(shape))
    page_spec = lambda index_map: pl.BlockSpec((None, width, page), index_map)
    new_spec = page_spec(lambda b, jj, pt: (b, 0, 0))
    page_specs = [page_spec(page_map(pg)) for pg in range(n_pages_step)]
    grid_spec = pltpu.PrefetchScalarGridSpec(
        num_scalar_prefetch=1,
        grid=(n_dec, n_steps),
        in_specs=[pl.BlockSpec((None, n_new, width), lambda b, jj, pt: (b, 0, 0)),
                  new_spec, new_spec,
                  const((n_cols, page)), const((page, page)), const((SUBLANES, n_cols))]
                 + page_specs + page_specs,
        out_specs=pl.BlockSpec((None, SUBLANES, width), lambda b, jj, pt: (b, 0, 0)),
        scratch_shapes=[pltpu.VMEM((n_cols, width), F32),
                        pltpu.VMEM((width, n_cols), F32),
                        pltpu.VMEM((n_cols, page), F32)],
    )
    return pl.pallas_call(
        functools.partial(_attn_sample_body, n_pages_step=n_pages_step, n_new=n_new),
        grid_spec=grid_spec,
        out_shape=jax.ShapeDtypeStruct((n_dec, SUBLANES, width), F32),
        compiler_params=_params("arbitrary", "arbitrary"),
        name="attn_sample",
    )(page_table.reshape(-1), q_bm, knt, vnt, bias_rep, tri, sel,
      *([cache_kt] * n_pages_step), *([cache_vt] * n_pages_step))


def _lru_gates(xc, wr_ref, br_ref, wi_ref, bi_ref, lam_ref):
    lam = lam_ref[...]
    neg_c_softplus = -LRU_C * (jnp.maximum(-lam, 0.0) + jnp.log1p(jnp.exp(-jnp.abs(lam))))
    xcb = xc.astype(BF16)
    a_parts, u_parts = [], []
    for n in range(xc.shape[1] // LRU_BLOCK):
        sl = slice(n * LRU_BLOCK, (n + 1) * LRU_BLOCK)
        r = jax.nn.sigmoid(jnp.dot(xcb[:, sl], wr_ref[n], preferred_element_type=F32) + br_ref[:, sl])
        ig = jax.nn.sigmoid(jnp.dot(xcb[:, sl], wi_ref[n], preferred_element_type=F32) + bi_ref[:, sl])
        a = jnp.exp(r * neg_c_softplus[:, sl])
        a_parts.append(a)
        u_parts.append(jnp.sqrt(1.0 - a * a) * (ig * xc[:, sl]))
    return a_parts, u_parts


def _lru_prompt_body(xl_ref, wc_ref, bc_ref, wr_ref, br_ref, wi_ref, bi_ref, lam_ref,
                     y_ref, hfin_ref, xbuf, h_scr, a_scr, u_scr, *, tb):
    t = pl.program_id(1)
    width = xl_ref.shape[1]
    pad = SUBLANES

    @pl.when(t == 0)
    def _():
        xbuf[0:pad, :] = jnp.zeros((pad, width), F32)
        h_scr[...] = jnp.zeros_like(h_scr)

    x = xl_ref[...]
    xbuf[pad:pad + tb, :] = x
    xc = bc_ref[...]
    for tap in range(CONV_W - 1):
        xc = xc + xbuf[pl.ds(pad - (CONV_W - 1) + tap, tb), :] * wc_ref[tap:tap + 1, :]
    xc = xc + x * wc_ref[CONV_W - 1:CONV_W, :]
    xbuf[0:pad, :] = x[tb - pad:, :]

    a_parts, u_parts = _lru_gates(xc, wr_ref, br_ref, wi_ref, bi_ref, lam_ref)
    for n, (a, u) in enumerate(zip(a_parts, u_parts)):
        sl = slice(n * LRU_BLOCK, (n + 1) * LRU_BLOCK)
        a_scr[:, sl] = a
        u_scr[:, sl] = u

    row = lax.broadcasted_iota(jnp.int32, (SUBLANES, width), 0)

    def group(g, h):
        r0 = pl.multiple_of(g * SUBLANES, SUBLANES)
        a = a_scr[pl.ds(r0, SUBLANES), :]
        u = u_scr[pl.ds(r0, SUBLANES), :]
        for d in (1, 2, 4):
            keep = row >= d
            a_prev = jnp.where(keep, pltpu.roll(a, d, 0), 1.0)
            u_prev = jnp.where(keep, pltpu.roll(u, d, 0), 0.0)
            u = u + a * u_prev
            a = a * a_prev
        hs = a * h + u
        u_scr[pl.ds(r0, SUBLANES), :] = hs
        return jnp.broadcast_to(hs[SUBLANES - 1:SUBLANES, :], (SUBLANES, width))

    h = lax.fori_loop(0, tb // SUBLANES, group, h_scr[...])
    h_scr[...] = h
    y_ref[...] = u_scr[...].astype(y_ref.dtype)
    hfin_ref[...] = h[:1, :]


def _lru_prompt(xl, wc, bc, wr_bf, br, wi_bf, bi, lam, n_seq, seq, tb):
    width = xl.shape[1]
    nt = seq // tb
    row_spec = pl.BlockSpec((1, width), lambda b, t: (0, 0))
    w_spec = pl.BlockSpec(wr_bf.shape, lambda b, t: (0, 0, 0))
    return pl.pallas_call(
        functools.partial(_lru_prompt_body, tb=tb),
        grid=(n_seq, nt),
        in_specs=[pl.BlockSpec((tb, width), lambda b, t: (b * nt + t, 0)),
                  pl.BlockSpec((CONV_W, width), lambda b, t: (0, 0)),
                  row_spec, w_spec, row_spec, w_spec, row_spec, row_spec],
        out_specs=[pl.BlockSpec((tb, width), lambda b, t: (b * nt + t, 0)),
                   pl.BlockSpec((None, 1, width), lambda b, t: (b, 0, 0))],
        out_shape=[jax.ShapeDtypeStruct(xl.shape, BF16),
                   jax.ShapeDtypeStruct((n_seq, 1, width), F32)],
        scratch_shapes=[pltpu.VMEM((tb + SUBLANES, width), F32),
                        pltpu.VMEM((SUBLANES, width), F32),
                        pltpu.VMEM((tb, width), F32),
                        pltpu.VMEM((tb, width), F32)],
        compiler_params=_params("arbitrary", "arbitrary"),
        name="lru_prompt",
    )(xl, wc, bc, wr_bf, br, wi_bf, bi, lam)


def _lru_sample_body(xl_ref, c0_ref, h0_ref, wc_ref, bc_ref, wr_ref, br_ref, wi_ref, bi_ref,
                     lam_ref, y_ref, hfin_ref):
    n_new = xl_ref.shape[0]
    hist = [c0_ref[i] for i in range(CONV_W - 1)] + [xl_ref[i] for i in range(n_new)]
    h = h0_ref[...]
    for t in range(n_new):
        xc = bc_ref[...]
        for tap in range(CONV_W):
            xc = xc + hist[tap + t] * wc_ref[tap:tap + 1, :]
        a_parts, u_parts = _lru_gates(xc, wr_ref, br_ref, wi_ref, bi_ref, lam_ref)
        h = jnp.concatenate(a_parts, axis=1) * h + jnp.concatenate(u_parts, axis=1)
        y_ref[t] = h.astype(y_ref.dtype)
    hfin_ref[...] = h


def _lru_sample(xl_tm, conv_tm, h0, wc, bc, wr_bf, br, wi_bf, bi, lam):
    n_new, n_dec, width = xl_tm.shape
    return pl.pallas_call(
        _lru_sample_body,
        out_shape=[jax.ShapeDtypeStruct((n_new, n_dec, width), BF16),
                   jax.ShapeDtypeStruct((n_dec, width), F32)],
        compiler_params=pltpu.CompilerParams(vmem_limit_bytes=VMEM_LIMIT_BYTES),
        name="lru_sample",
    )(xl_tm, conv_tm, h0, wc, bc, wr_bf, br, wi_bf, bi, lam)


def _merge_body(x_ref, oa_ref, yl_ref, ga_ref, gb_ref, wa_ref, wb_ref, wo_ref, g_ref,
                x1_ref, h2_ref):
    pa = jnp.dot(oa_ref[...], wa_ref[...], preferred_element_type=F32)
    pb = jnp.dot(yl_ref[...], wb_ref[...], preferred_element_type=F32)
    mixed = ga_ref[...] * pa + gb_ref[...] * pb
    x1 = x_ref[...] + jnp.dot(mixed.astype(BF16), wo_ref[...], preferred_element_type=F32)
    x1_ref[...] = x1
    h2_ref[...] = _rmsnorm(x1, g_ref[...]).astype(BF16)


def _merge(x, o_att, y_lru, ga, gb, wa_bf, wb_bf, wo_bf, g_ffn, tm):
    m, d = x.shape
    tile = pl.BlockSpec((tm, d), lambda i: (i, 0))
    w_spec = pl.BlockSpec((d, d), lambda i: (0, 0))
    return pl.pallas_call(
        _merge_body,
        grid=(m // tm,),
        in_specs=[tile, tile, tile, tile, tile, w_spec, w_spec, w_spec,
                  pl.BlockSpec((1, d), lambda i: (0, 0))],
        out_specs=[tile, tile],
        out_shape=[jax.ShapeDtypeStruct((m, d), F32), jax.ShapeDtypeStruct((m, d), BF16)],
        compiler_params=_params("arbitrary"),
        name="merge",
    )(x, o_att, y_lru, ga, gb, wa_bf, wb_bf, wo_bf, g_ffn)


def _ffn_body(x1_ref, h2_ref, wg_ref, wu_ref, wd_ref, g_ref, y_ref, acc_ref, *, final_norm):
    f = pl.program_id(1)

    @pl.when(f == 0)
    def _():
        acc_ref[...] = jnp.zeros_like(acc_ref)

    h2 = h2_ref[...]
    gate = jnp.dot(h2, wg_ref[...], preferred_element_type=F32)
    up = jnp.dot(h2, wu_ref[...], preferred_element_type=F32)
    act = (jax.nn.silu(gate) * up).astype(BF16)
    acc_ref[...] += jnp.dot(act, wd_ref[...], preferred_element_type=F32)

    @pl.when(f == pl.num_programs(1) - 1)
    def _():
        x2 = x1_ref[...] + acc_ref[...]
        y_ref[...] = _rmsnorm(x2, g_ref[...]) if final_norm else x2


def _ffn(x1, h2, wg_bf, wu_bf, wd_bf, g_final, final_norm, tm, tf):
    m, d = x1.shape
    d_ff = wg_bf.shape[1]
    tile = pl.BlockSpec((tm, d), lambda i, f: (i, 0))
    return pl.pallas_call(
        functools.partial(_ffn_body, final_norm=final_norm),
        grid=(m // tm, d_ff // tf),
        in_specs=[tile, tile,
                  pl.BlockSpec((d, tf), lambda i, f: (0, f)),
                  pl.BlockSpec((d, tf), lambda i, f: (0, f)),
                  pl.BlockSpec((tf, d), lambda i, f: (f, 0)),
                  pl.BlockSpec((1, d), lambda i, f: (0, 0))],
        out_specs=tile,
        out_shape=jax.ShapeDtypeStruct((m, d), F32),
        scratch_shapes=[pltpu.VMEM((tm, d), F32)],
        compiler_params=_params("arbitrary", "arbitrary"),
        name="ffn",
    )(x1, h2, wg_bf, wu_bf, wd_bf, g_final)


TOKEN_TILE = 512
FFN_TILE = 1408
ATTN_TILE = 256
LRU_TIME_TILE = 512
PAGES_PER_STEP = 16
SAMPLE_PAGE_GROUP = 16


def kernel(x_prompt, x_sample, cache_k, cache_v, state_conv, state_lru, page_table, g_mix_norm, w_in, b_att, w_conv, b_conv, w_r, b_r, w_i, b_i, lam, w_branch_a, w_branch_b, w_o, g_ffn_norm, w_gate, w_up, w_down, g_final):
    n_prompt, seq, d = x_prompt.shape
    n_dec, n_new, _ = x_sample.shape
    depth = w_in.shape[0]
    n_pool, page = cache_k.shape[1], cache_k.shape[2]
    width = N_HEADS * HEAD_DIM
    lru_width = w_conv.shape[2]
    assert d == width == lru_width, "kernels assume one shared channel width"

    xp = x_prompt.reshape(n_prompt * seq, d)
    xs = x_sample.transpose(1, 0, 2).reshape(n_new * n_dec, d)
    row = lambda v: v.reshape(1, -1)
    outs = [[] for _ in range(8)]
    for l in range(depth):
        last = l == depth - 1
        w_in_bf = w_in[l].astype(BF16)
        wr_bf, wi_bf = w_r[l].astype(BF16), w_i[l].astype(BF16)
        wa_bf, wb_bf, wo_bf = (w_branch_a[l].astype(BF16), w_branch_b[l].astype(BF16),
                               w_o[l].astype(BF16))
        wg_bf, wu_bf, wd_bf = w_gate[l].astype(BF16), w_up[l].astype(BF16), w_down[l].astype(BF16)
        bias2 = b_att[l].astype(F32) * LOG2E
        lru_w = (w_conv[l], row(b_conv[l]), wr_bf, row(b_r[l]), wi_bf, row(b_i[l]), row(lam[l]))
        g_next = row(g_final)

        k_p, v_p, qs, kb, vb, xl, ga, gb = _in_proj(xp, row(g_mix_norm[l]), w_in_bf, TOKEN_TILE)
        o_att = _attn_prompt(qs, kb, vb, bias2, n_prompt, seq, ATTN_TILE)
        y_lru, h_p = _lru_prompt(xl, *lru_w, n_prompt, seq, LRU_TIME_TILE)
        x1, h2 = _merge(xp, o_att, y_lru, ga, gb, wa_bf, wb_bf, wo_bf, row(g_ffn_norm[l]), TOKEN_TILE)
        xp = _ffn(x1, h2, wg_bf, wu_bf, wd_bf, g_next, last, TOKEN_TILE, FFN_TILE)
        outs[0].append(k_p.reshape(n_prompt, seq, N_HEADS, HEAD_DIM))
        outs[1].append(v_p.reshape(n_prompt, seq, N_HEADS, HEAD_DIM))
        outs[2].append(xl.reshape(n_prompt, seq, lru_width)[:, seq - (CONV_W - 1):])
        outs[3].append(h_p.reshape(n_prompt, lru_width))

        k_s, v_s, qs, _, _, xl, ga, gb = _in_proj(xs, row(g_mix_norm[l]), w_in_bf, n_new * n_dec)
        to_bm = lambda v: v.reshape(n_new, n_dec, -1).transpose(1, 0, 2)
        k_bm, v_bm = to_bm(k_s), to_bm(v_s)
        bias_cols = jnp.pad(jnp.repeat(bias2, n_new), (0, LANES - N_HEADS * n_new))
        bias_rep = jnp.broadcast_to(bias_cols[:, None], (LANES, page))
        to_pages_t = lambda c: c.transpose(0, 2, 3, 1).reshape(c.shape[0], width, page)
        new_page_t = lambda v: jnp.pad(v.transpose(0, 2, 1), ((0, 0), (0, 0), (0, page - n_new)))
        o8 = _attn_sample(to_bm(qs), new_page_t(k_bm), new_page_t(v_bm), bias_rep,
                          to_pages_t(cache_k[l]), to_pages_t(cache_v[l]), page_table, PAGES_PER_STEP)
        o_att = o8[:, :n_new].transpose(1, 0, 2).reshape(n_new * n_dec, width).astype(BF16)
        xl_tm = xl.reshape(n_new, n_dec, lru_width)
        conv_tm = state_conv[l].transpose(1, 0, 2)
        y_lru, h_s = _lru_sample(xl_tm, conv_tm, state_lru[l], *lru_w)
        x1, h2 = _merge(xs, o_att, y_lru.reshape(n_new * n_dec, lru_width), ga, gb,
                        wa_bf, wb_bf, wo_bf, row(g_ffn_norm[l]), n_new * n_dec)
        xs = _ffn(x1, h2, wg_bf, wu_bf, wd_bf, g_next, last, n_new * n_dec, FFN_TILE)
        hist_tm = jnp.concatenate([conv_tm, xl_tm], axis=0)
        outs[4].append(k_bm.reshape(n_dec, n_new, N_HEADS, HEAD_DIM))
        outs[5].append(v_bm.reshape(n_dec, n_new, N_HEADS, HEAD_DIM))
        outs[6].append(hist_tm[n_new:n_new + CONV_W - 1].transpose(1, 0, 2))
        outs[7].append(h_s)

    y_prompt = xp.reshape(n_prompt, seq, d)
    y_sample = xs.reshape(n_new, n_dec, d).transpose(1, 0, 2)
    return (y_prompt, y_sample) + tuple(jnp.stack(o) for o in outs)
```

```python
import functools

import jax
import jax.numpy as jnp
from jax import lax
from jax.experimental import pallas as pl
from jax.experimental.pallas import tpu as pltpu

F32 = jnp.float32
BF16 = jnp.bfloat16

LANES = 128
SUBLANES = 8
VMEM_LIMIT_BYTES = 56 * 1024 * 1024

HEAD_DIM = 64
N_HEADS = 16
LRU_BLOCK = 128
CONV_W = 4
LRU_C = 8.0
EPS = 1e-6
LOG2E = 1.4426950408889634
LN2 = 0.6931471805599453

NT_DIMS = (((1,), (1,)), ((), ()))
TN_DIMS = (((0,), (0,)), ((), ()))


def _params(*sem):
    return pltpu.CompilerParams(dimension_semantics=sem, vmem_limit_bytes=VMEM_LIMIT_BYTES)


def _rmsnorm(x, g):
    ms = jnp.mean(x * x, axis=-1, keepdims=True)
    return x * lax.rsqrt(ms + EPS) * g


EXP2_ARG_MAX = 126.0


def _sp2(z):
    return jnp.maximum(z, jnp.log(1.0 + jnp.exp2(jnp.minimum(z, EXP2_ARG_MAX))) * LOG2E)


def _in_proj_body(x_ref, g_ref, w_ref, k_ref, v_ref, qs_ref, kb_ref, vb_ref, xl_ref,
                  ga_ref, gb_ref, h_scr, *, kv_transposed):
    j = pl.program_id(1)
    kv_out = (lambda y: y.T) if kv_transposed else (lambda y: y)

    @pl.when(j == 0)
    def _():
        h_scr[...] = _rmsnorm(x_ref[...], g_ref[...]).astype(BF16)

    y = jnp.dot(h_scr[...], w_ref[...], preferred_element_type=F32)

    @pl.when(j == 0)
    def _():
        qs_ref[...] = (y * (HEAD_DIM ** -0.5 * LOG2E)).astype(BF16)

    @pl.when(j == 1)
    def _():
        k_ref[...] = kv_out(y)
        kb_ref[...] = y.astype(BF16)

    @pl.when(j == 2)
    def _():
        v_ref[...] = kv_out(y)
        vb_ref[...] = y.astype(BF16)

    @pl.when(j == 3)
    def _():
        xl_ref[...] = y

    @pl.when(j == 4)
    def _():
        ga_ref[...] = jax.nn.sigmoid(y)

    @pl.when(j == 5)
    def _():
        gb_ref[...] = jax.nn.sigmoid(y)


def _in_proj(x, g, w_bf, tm, kv_seq=None):
    m, d = x.shape
    n_groups = w_bf.shape[1] // d
    out_spec = pl.BlockSpec((tm, d), lambda i, j: (i, 0))
    f32o = jax.ShapeDtypeStruct((m, d), F32)
    bf16o = jax.ShapeDtypeStruct((m, d), BF16)
    kv_spec, kvo = out_spec, f32o
    if kv_seq is not None:
        steps = kv_seq // tm
        kv_spec = pl.BlockSpec((None, d, tm), lambda i, j: (i // steps, 0, i % steps))
        kvo = jax.ShapeDtypeStruct((m // kv_seq, d, kv_seq), F32)
    return pl.pallas_call(
        functools.partial(_in_proj_body, kv_transposed=kv_seq is not None),
        grid=(m // tm, n_groups),
        in_specs=[pl.BlockSpec((tm, d), lambda i, j: (i, 0)),
                  pl.BlockSpec((1, d), lambda i, j: (0, 0)),
                  pl.BlockSpec((d, d), lambda i, j: (0, j))],
        out_specs=[kv_spec, kv_spec] + [out_spec] * 6,
        out_shape=[kvo, kvo, bf16o, bf16o, bf16o, f32o, f32o, f32o],
        scratch_shapes=[pltpu.VMEM((tm, d), BF16)],
        compiler_params=_params("arbitrary", "arbitrary"),
        name="in_proj",
    )(x, g, w_bf)


ATTN_STAGES = 3
MASKED_LOGIT = -1e30


def _attn_prompt_body(it_ref, jt_ref, b_ref, q_ref, k_ref, v_ref, tri_ref, o_ref,
                      acc_ref, c_ref, z_ref, a_ref, *, tq, tk, n_items):
    p = pl.program_id(1)
    bias = (b_ref[2 * p], b_ref[2 * p + 1])
    lane = lax.broadcasted_iota(jnp.int32, (tq, LANES), 1)
    depth = ATTN_STAGES - 1
    n_slots = z_ref.shape[0]
    for ref in (acc_ref, c_ref, z_ref, a_ref):
        ref[...] = jnp.zeros_like(ref)

    def body(n, carry):
        i_out, j_out = it_ref[n], jt_ref[n]
        i_w, j_w = it_ref[n + 1], jt_ref[n + 1]
        i_new, j_new = it_ref[n + depth], jt_ref[n + depth]

        vj = v_ref[pl.ds(pl.multiple_of(j_out * tk, tk), tk), :]
        for hh in range(2):
            prev = jnp.where(j_out == i_out, 0.0, acc_ref[hh])
            acc_ref[hh] = prev + jnp.dot(a_ref[hh], vj, preferred_element_type=F32)

        zs = z_ref[(n + n_slots - 1) % n_slots]
        for hh in range(2):
            z = zs[hh]
            cum = jnp.dot(_sp2(z).astype(BF16), tri_ref[...], preferred_element_type=F32)
            c = jnp.where(j_w == i_w, 0.0, c_ref[hh])
            a = jnp.exp2(z + cum + jnp.concatenate([c] * (tk // LANES), axis=1))
            a_ref[hh] = a.astype(BF16)
            c_ref[hh] = c + jnp.broadcast_to(cum[:, :1], (tq, LANES))

        q = q_ref[pl.ds(pl.multiple_of(i_new * tq, tq), tq), :]
        kj = k_ref[pl.ds(pl.multiple_of(j_new * tk, tk), tk), :]
        zero = jnp.zeros_like(q)
        slot = n % n_slots
        q_heads = (jnp.where(lane < HEAD_DIM, q, zero), jnp.where(lane >= HEAD_DIM, q, zero))
        for hh in range(2):
            z = lax.dot_general(q_heads[hh], kj, NT_DIMS, preferred_element_type=F32)
            z_ref[slot, hh] = z + bias[hh]

        @pl.when(j_new == i_new)
        def _():
            row = lax.broadcasted_iota(jnp.int32, (tq, tk), 0)
            col = lax.broadcasted_iota(jnp.int32, (tq, tk), 1)
            for hh in range(2):
                z_ref[slot, hh] = jnp.where(col < row, z_ref[slot, hh], MASKED_LOGIT)

        @pl.when(j_out == 0)
        def _():
            out = jnp.where(lane < HEAD_DIM, acc_ref[0], acc_ref[1])
            o_ref[pl.ds(pl.multiple_of(i_out * tq, tq), tq), :] = out.astype(o_ref.dtype)

        return carry

    lax.fori_loop(0, n_items + depth, body, 0)


def _attn_prompt(qs, kb, vb, bias2, n_seq, seq, tq):
    tk = tq
    width = qs.shape[1]
    n_pairs = width // LANES
    nq = seq // tq
    depth = ATTN_STAGES - 1
    items = [(i, j) for i in range(nq) for j in range(i, -1, -1)]
    padded = [(0, 0)] * depth + items + [(nq - 1, 0)] * depth
    it = jnp.asarray([i for i, _ in padded], jnp.int32)
    jt = jnp.asarray([j for _, j in padded], jnp.int32)
    r = lax.broadcasted_iota(jnp.int32, (tk, tk), 0)
    c = lax.broadcasted_iota(jnp.int32, (tk, tk), 1)
    tri = jnp.where(r >= c, -1.0, 0.0).astype(BF16)
    seq_spec = pl.BlockSpec((seq, LANES), lambda b, p, it, jt: (b, p))
    grid_spec = pltpu.PrefetchScalarGridSpec(
        num_scalar_prefetch=2,
        grid=(n_seq, n_pairs),
        in_specs=[pl.BlockSpec(memory_space=pltpu.SMEM), seq_spec, seq_spec, seq_spec,
                  pl.BlockSpec((tk, tk), lambda b, p, it, jt: (0, 0))],
        out_specs=seq_spec,
        scratch_shapes=[pltpu.VMEM((2, tq, LANES), F32), pltpu.VMEM((2, tq, LANES), F32),
                        pltpu.VMEM((depth, 2, tq, tk), F32), pltpu.VMEM((2, tq, tk), BF16)],
    )
    return pl.pallas_call(
        functools.partial(_attn_prompt_body, tq=tq, tk=tk, n_items=len(items)),
        grid_spec=grid_spec,
        out_shape=jax.ShapeDtypeStruct(qs.shape, BF16),
        compiler_params=_params("arbitrary", "arbitrary"),
        name="attn_prompt",
    )(it, jt, bias2, qs, kb, vb, tri)


def _attn_sample_body(pt_ref, q_ref, knt_ref, vnt_ref, bias_ref, tri_ref, sel_ref, *rest,
                      n_pages_step, n_new):
    del pt_ref
    kt_refs = rest[:n_pages_step]
    vt_refs = rest[n_pages_step:2 * n_pages_step]
    o_ref, qbd_ref, acc_ref, c_ref = rest[2 * n_pages_step:]
    jj = pl.program_id(1)
    n_cols, width = qbd_ref.shape
    page = tri_ref.shape[0]

    def keys_block(kt_list, vt_list, c, mask):
        n = len(kt_list)
        cat = lambda xs: xs[0] if len(xs) == 1 else jnp.concatenate(xs, axis=1)
        z = jnp.dot(qbd_ref[...], cat(kt_list), preferred_element_type=F32) + cat([bias_ref[...]] * n)
        sp = _sp2(z)
        if mask is not None:
            sp = jnp.where(mask, sp, 0.0)
        a_chunks = [None] * n
        for ch in reversed(range(n)):
            sl = slice(ch * page, (ch + 1) * page)
            cum = jnp.dot(sp[:, sl].astype(BF16), tri_ref[...], preferred_element_type=F32)
            a = jnp.exp2(z[:, sl] + cum + c)
            if mask is not None:
                a = jnp.where(mask, a, 0.0)
            a_chunks[ch] = a
            c = c + jnp.broadcast_to(cum[:, :1], c.shape)
        return c, lax.dot_general(cat(vt_list), cat(a_chunks), NT_DIMS, preferred_element_type=F32)

    @pl.when(jj == 0)
    def _():
        r = lax.broadcasted_iota(jnp.int32, (n_cols, width), 0)
        l = lax.broadcasted_iota(jnp.int32, (n_cols, width), 1)
        on_head = (l // HEAD_DIM == r // n_new) & (r < N_HEADS * n_new)
        q = q_ref[...].astype(F32)
        qbd = jnp.zeros((n_cols, width), F32)
        for t in range(n_new):
            qbd = jnp.where(on_head & (r % n_new == t), q[t:t + 1, :], qbd)
        qbd_ref[...] = qbd
        col = lax.broadcasted_iota(jnp.int32, (n_cols, page), 0)
        key = lax.broadcasted_iota(jnp.int32, (n_cols, page), 1)
        c_ref[...], acc_ref[...] = keys_block([knt_ref[...]], [vnt_ref[...]],
                                              jnp.zeros(c_ref.shape, F32), key < col % n_new)

    c = c_ref[...]
    out = None
    for g in reversed(range(n_pages_step // SAMPLE_PAGE_GROUP)):
        refs = slice(g * SAMPLE_PAGE_GROUP, (g + 1) * SAMPLE_PAGE_GROUP)
        c, part = keys_block([ref[...] for ref in kt_refs[refs]], [ref[...] for ref in vt_refs[refs]],
                             c, None)
        out = part if out is None else out + part
    c_ref[...] = c
    acc_ref[...] += out

    @pl.when(jj == pl.num_programs(1) - 1)
    def _():
        ch = lax.broadcasted_iota(jnp.int32, (width, n_cols), 0)
        col = lax.broadcasted_iota(jnp.int32, (width, n_cols), 1)
        own = jnp.where(ch // HEAD_DIM == col // n_new, acc_ref[...], 0.0)
        o_ref[...] = lax.dot_general(sel_ref[...], own.astype(BF16), NT_DIMS,
                                     preferred_element_type=F32)


def _attn_sample(q_bm, knt, vnt, bias_rep, cache_kt, cache_vt, page_table, n_pages_step):
    n_dec, n_new, width = q_bm.shape
    n_pages = page_table.shape[1]
    page = cache_kt.shape[2]
    n_steps = n_pages // n_pages_step
    n_cols = LANES
    rr = lax.broadcasted_iota(jnp.int32, (page, page), 0)
    cc = lax.broadcasted_iota(jnp.int32, (page, page), 1)
    tri = jnp.where(rr >= cc, -1.0, 0.0).astype(BF16)
    sr = lax.broadcasted_iota(jnp.int32, (SUBLANES, n_cols), 0)
    sc = lax.broadcasted_iota(jnp.int32, (SUBLANES, n_cols), 1)
    sel = jnp.where((sc % n_new == sr) & (sc < N_HEADS * n_new) & (sr < n_new), 1.0, 0.0).astype(BF16)

    def page_map(pg):
        def index_map(b, jj, pt):
            logical = (n_steps - 1 - jj) * n_pages_step + pg
            return (pt[b * n_pages + logical], 0, 0)
        return index_map

    const = lambda shape: pl.BlockSpec(shape, lambda b, jj, pt: (0,) * len(shape))
    page_spec = lambda index_map: pl.BlockSpec((None, width, page), index_map)
    new_spec = page_spec(lambda b, jj, pt: (b, 0, 0))
    page_specs = [page_spec(page_map(pg)) for pg in range(n_pages_step)]
    grid_spec = pltpu.PrefetchScalarGridSpec(
        num_scalar_prefetch=1,
        grid=(n_dec, n_steps),
        in_specs=[pl.BlockSpec((None, n_new, width), lambda b, jj, pt: (b, 0, 0)),
                  new_spec, new_spec,
                  const((n_cols, page)), const((page, page)), const((SUBLANES, n_cols))]
                 + page_specs + page_specs,
        out_specs=pl.BlockSpec((None, SUBLANES, width), lambda b, jj, pt: (b, 0, 0)),
        scratch_shapes=[pltpu.VMEM((n_cols, width), F32),
                        pltpu.VMEM((width, n_cols), F32),
                        pltpu.VMEM((n_cols, page), F32)],
    )
    return pl.pallas_call(
        functools.partial(_attn_sample_body, n_pages_step=n_pages_step, n_new=n_new),
        grid_spec=grid_spec,
        out_shape=jax.ShapeDtypeStruct((n_dec, SUBLANES, width), F32),
        compiler_params=_params("arbitrary", "arbitrary"),
        name="attn_sample",
    )(page_table.reshape(-1), q_bm, knt, vnt, bias_rep, tri, sel,
      *([cache_kt] * n_pages_step), *([cache_vt] * n_pages_step))


def _lru_gates(xc, wr_ref, br_ref, wi_ref, bi_ref, lam_ref):
    lam = lam_ref[...]
    neg_c_softplus = -LRU_C * (jnp.maximum(-lam, 0.0) + jnp.log1p(jnp.exp(-jnp.abs(lam))))
    xcb = xc.astype(BF16)
    a_parts, u_parts = [], []
    for n in range(xc.shape[1] // LRU_BLOCK):
        sl = slice(n * LRU_BLOCK, (n + 1) * LRU_BLOCK)
        r = jax.nn.sigmoid(jnp.dot(xcb[:, sl], wr_ref[n], preferred_element_type=F32) + br_ref[:, sl])
        ig = jax.nn.sigmoid(jnp.dot(xcb[:, sl], wi_ref[n], preferred_element_type=F32) + bi_ref[:, sl])
        a = jnp.exp(r * neg_c_softplus[:, sl])
        a_parts.append(a)
        u_parts.append(jnp.sqrt(1.0 - a * a) * (ig * xc[:, sl]))
    return a_parts, u_parts


def _lru_prompt_body(xl_ref, wc_ref, bc_ref, wr_ref, br_ref, wi_ref, bi_ref, lam_ref,
                     y_ref, hfin_ref, xbuf, h_scr, a_scr, u_scr, *, tb):
    t = pl.program_id(1)
    width = xl_ref.shape[1]
    pad = SUBLANES

    @pl.when(t == 0)
    def _():
        xbuf[0:pad, :] = jnp.zeros((pad, width), F32)
        h_scr[...] = jnp.zeros_like(h_scr)

    x = xl_ref[...]
    xbuf[pad:pad + tb, :] = x
    xc = bc_ref[...]
    for tap in range(CONV_W - 1):
        xc = xc + xbuf[pl.ds(pad - (CONV_W - 1) + tap, tb), :] * wc_ref[tap:tap + 1, :]
    xc = xc + x * wc_ref[CONV_W - 1:CONV_W, :]
    xbuf[0:pad, :] = x[tb - pad:, :]

    a_parts, u_parts = _lru_gates(xc, wr_ref, br_ref, wi_ref, bi_ref, lam_ref)
    for n, (a, u) in enumerate(zip(a_parts, u_parts)):
        sl = slice(n * LRU_BLOCK, (n + 1) * LRU_BLOCK)
        a_scr[:, sl] = a
        u_scr[:, sl] = u

    row = lax.broadcasted_iota(jnp.int32, (SUBLANES, width), 0)

    def group(g, h):
        r0 = pl.multiple_of(g * SUBLANES, SUBLANES)
        a = a_scr[pl.ds(r0, SUBLANES), :]
        u = u_scr[pl.ds(r0, SUBLANES), :]
        for d in (1, 2, 4):
            keep = row >= d
            a_prev = jnp.where(keep, pltpu.roll(a, d, 0), 1.0)
            u_prev = jnp.where(keep, pltpu.roll(u, d, 0), 0.0)
            u = u + a * u_prev
            a = a * a_prev
        hs = a * h + u
        u_scr[pl.ds(r0, SUBLANES), :] = hs
        return jnp.broadcast_to(hs[SUBLANES - 1:SUBLANES, :], (SUBLANES, width))

    h = lax.fori_loop(0, tb // SUBLANES, group, h_scr[...])
    h_scr[...] = h
    y_ref[...] = u_scr[...].astype(y_ref.dtype)
    hfin_ref[...] = h[:1, :]


def _lru_prompt(xl, wc, bc, wr_bf, br, wi_bf, bi, lam, n_seq, seq, tb):
    width = xl.shape[1]
    nt = seq // tb
    row_spec = pl.BlockSpec((1, width), lambda b, t: (0, 0))
    w_spec = pl.BlockSpec(wr_bf.shape, lambda b, t: (0, 0, 0))
    return pl.pallas_call(
        functools.partial(_lru_prompt_body, tb=tb),
        grid=(n_seq, nt),
        in_specs=[pl.BlockSpec((tb, width), lambda b, t: (b * nt + t, 0)),
                  pl.BlockSpec((CONV_W, width), lambda b, t: (0, 0)),
                  row_spec, w_spec, row_spec, w_spec, row_spec, row_spec],
        out_specs=[pl.BlockSpec((tb, width), lambda b, t: (b * nt + t, 0)),
                   pl.BlockSpec((None, 1, width), lambda b, t: (b, 0, 0))],
        out_shape=[jax.ShapeDtypeStruct(xl.shape, BF16),
                   jax.ShapeDtypeStruct((n_seq, 1, width), F32)],
        scratch_shapes=[pltpu.VMEM((tb + SUBLANES, width), F32),
                        pltpu.VMEM((SUBLANES, width), F32),
                        pltpu.VMEM((tb, width), F32),
                        pltpu.VMEM((tb, width), F32)],
        compiler_params=_params("arbitrary", "arbitrary"),
        name="lru_prompt",
    )(xl, wc, bc, wr_bf, br, wi_bf, bi, lam)


def _lru_sample_body(xl_ref, c0_ref, h0_ref, wc_ref, bc_ref, wr_ref, br_ref, wi_ref, bi_ref,
                     lam_ref, y_ref, hfin_ref):
    n_new = xl_ref.shape[0]
    hist = [c0_ref[i] for i in range(CONV_W - 1)] + [xl_ref[i] for i in range(n_new)]
    h = h0_ref[...]
    for t in range(n_new):
        xc = bc_ref[...]
        for tap in range(CONV_W):
            xc = xc + hist[tap + t] * wc_ref[tap:tap + 1, :]
        a_parts, u_parts = _lru_gates(xc, wr_ref, br_ref, wi_ref, bi_ref, lam_ref)
        h = jnp.concatenate(a_parts, axis=1) * h + jnp.concatenate(u_parts, axis=1)
        y_ref[t] = h.astype(y_ref.dtype)
    hfin_ref[...] = h


def _lru_sample(xl_tm, conv_tm, h0, wc, bc, wr_bf, br, wi_bf, bi, lam):
    n_new, n_dec, width = xl_tm.shape
    return pl.pallas_call(
        _lru_sample_body,
        out_shape=[jax.ShapeDtypeStruct((n_new, n_dec, width), BF16),
                   jax.ShapeDtypeStruct((n_dec, width), F32)],
        compiler_params=pltpu.CompilerParams(vmem_limit_bytes=VMEM_LIMIT_BYTES),
        name="lru_sample",
    )(xl_tm, conv_tm, h0, wc, bc, wr_bf, br, wi_bf, bi, lam)


def _merge_body(x_ref, oa_ref, yl_ref, ga_ref, gb_ref, wa_ref, wb_ref, wo_ref, g_ref,
                x1_ref, h2_ref):
    pa = jnp.dot(oa_ref[...], wa_ref[...], preferred_element_type=F32)
    pb = jnp.dot(yl_ref[...], wb_ref[...], preferred_element_type=F32)
    mixed = ga_ref[...] * pa + gb_ref[...] * pb
    x1 = x_ref[...] + jnp.dot(mixed.astype(BF16), wo_ref[...], preferred_element_type=F32)
    x1_ref[...] = x1
    h2_ref[...] = _rmsnorm(x1, g_ref[...]).astype(BF16)


def _merge(x, o_att, y_lru, ga, gb, wa_bf, wb_bf, wo_bf, g_ffn, tm):
    m, d = x.shape
    tile = pl.BlockSpec((tm, d), lambda i: (i, 0))
    w_spec = pl.BlockSpec((d, d), lambda i: (0, 0))
    return pl.pallas_call(
        _merge_body,
        grid=(m // tm,),
        in_specs=[tile, tile, tile, tile, tile, w_spec, w_spec, w_spec,
                  pl.BlockSpec((1, d), lambda i: (0, 0))],
        out_specs=[tile, tile],
        out_shape=[jax.ShapeDtypeStruct((m, d), F32), jax.ShapeDtypeStruct((m, d), BF16)],
        compiler_params=_params("arbitrary"),
        name="merge",
    )(x, o_att, y_lru, ga, gb, wa_bf, wb_bf, wo_bf, g_ffn)


def _ffn_body(x1_ref, h2_ref, wg_ref, wu_ref, wd_ref, g_ref, y_ref, acc_ref, *, final_norm):
    f = pl.program_id(1)

    @pl.when(f == 0)
    def _():
        acc_ref[...] = jnp.zeros_like(acc_ref)

    h2 = h2_ref[...]
    gate = jnp.dot(h2, wg_ref[...], preferred_element_type=F32)
    up = jnp.dot(h2, wu_ref[...], preferred_element_type=F32)
    act = (jax.nn.silu(gate) * up).astype(BF16)
    acc_ref[...] += jnp.dot(act, wd_ref[...], preferred_element_type=F32)

    @pl.when(f == pl.num_programs(1) - 1)
    def _():
        x2 = x1_ref[...] + acc_ref[...]
        y_ref[...] = _rmsnorm(x2, g_ref[...]) if final_norm else x2


def _ffn(x1, h2, wg_bf, wu_bf, wd_bf, g_final, final_norm, tm, tf):
    m, d = x1.shape
    d_ff = wg_bf.shape[1]
    tile = pl.BlockSpec((tm, d), lambda i, f: (i, 0))
    return pl.pallas_call(
        functools.partial(_ffn_body, final_norm=final_norm),
        grid=(m // tm, d_ff // tf),
        in_specs=[tile, tile,
                  pl.BlockSpec((d, tf), lambda i, f: (0, f)),
                  pl.BlockSpec((d, tf), lambda i, f: (0, f)),
                  pl.BlockSpec((tf, d), lambda i, f: (f, 0)),
                  pl.BlockSpec((1, d), lambda i, f: (0, 0))],
        out_specs=tile,
        out_shape=jax.ShapeDtypeStruct((m, d), F32),
        scratch_shapes=[pltpu.VMEM((tm, d), F32)],
        compiler_params=_params("arbitrary", "arbitrary"),
        name="ffn",
    )(x1, h2, wg_bf, wu_bf, wd_bf, g_final)


TOKEN_TILE = 512
FFN_TILE = 1408
ATTN_TILE = 256
LRU_TIME_TILE = 512
PAGES_PER_STEP = 16
SAMPLE_PAGE_GROUP = 16


def kernel(x_prompt, x_sample, cache_k, cache_v, state_conv, state_lru, page_table, g_mix_norm, w_in, b_att, w_conv, b_conv, w_r, b_r, w_i, b_i, lam, w_branch_a, w_branch_b, w_o, g_ffn_norm, w_gate, w_up, w_down, g_final):
    n_prompt, seq, d = x_prompt.shape
    n_dec, n_new, _ = x_sample.shape
    depth = w_in.shape[0]
    n_pool, page = cache_k.shape[1], cache_k.shape[2]
    width = N_HEADS * HEAD_DIM
    lru_width = w_conv.shape[2]
    assert d == width == lru_width, "kernels assume one shared channel width"

    xp = x_prompt.reshape(n_prompt * seq, d)
    xs = x_sample.transpose(1, 0, 2).reshape(n_new * n_dec, d)
    row = lambda v: v.reshape(1, -1)
    outs = [[] for _ in range(8)]
    for l in range(depth):
        last = l == depth - 1
        w_in_bf = w_in[l].astype(BF16)
        wr_bf, wi_bf = w_r[l].astype(BF16), w_i[l].astype(BF16)
        wa_bf, wb_bf, wo_bf = (w_branch_a[l].astype(BF16), w_branch_b[l].astype(BF16),
                               w_o[l].astype(BF16))
        wg_bf, wu_bf, wd_bf = w_gate[l].astype(BF16), w_up[l].astype(BF16), w_down[l].astype(BF16)
        bias2 = b_att[l].astype(F32) * LOG2E
        lru_w = (w_conv[l], row(b_conv[l]), wr_bf, row(b_r[l]), wi_bf, row(b_i[l]), row(lam[l]))
        g_next = row(g_final)

        k_p, v_p, qs, kb, vb, xl, ga, gb = _in_proj(xp, row(g_mix_norm[l]), w_in_bf, TOKEN_TILE, seq)
        o_att = _attn_prompt(qs, kb, vb, bias2, n_prompt, seq, ATTN_TILE)
        y_lru, h_p = _lru_prompt(xl, *lru_w, n_prompt, seq, LRU_TIME_TILE)
        x1, h2 = _merge(xp, o_att, y_lru, ga, gb, wa_bf, wb_bf, wo_bf, row(g_ffn_norm[l]), TOKEN_TILE)
        xp = _ffn(x1, h2, wg_bf, wu_bf, wd_bf, g_next, last, TOKEN_TILE, FFN_TILE)
        kv_view = lambda t: t.reshape(n_prompt, N_HEADS, HEAD_DIM, seq).transpose(0, 3, 1, 2)
        outs[0].append(kv_view(k_p))
        outs[1].append(kv_view(v_p))
        outs[2].append(xl.reshape(n_prompt, seq, lru_width)[:, seq - (CONV_W - 1):])
        outs[3].append(h_p.reshape(n_prompt, lru_width))

        k_s, v_s, qs, _, _, xl, ga, gb = _in_proj(xs, row(g_mix_norm[l]), w_in_bf, n_new * n_dec)
        to_bm = lambda v: v.reshape(n_new, n_dec, -1).transpose(1, 0, 2)
        k_bm, v_bm = to_bm(k_s), to_bm(v_s)
        bias_cols = jnp.pad(jnp.repeat(bias2, n_new), (0, LANES - N_HEADS * n_new))
        bias_rep = jnp.broadcast_to(bias_cols[:, None], (LANES, page))
        to_pages_t = lambda c: c.transpose(0, 2, 3, 1).reshape(c.shape[0], width, page)
        new_page_t = lambda v: jnp.pad(v.transpose(0, 2, 1), ((0, 0), (0, 0), (0, page - n_new)))
        o8 = _attn_sample(to_bm(qs), new_page_t(k_bm), new_page_t(v_bm), bias_rep,
                          to_pages_t(cache_k[l]), to_pages_t(cache_v[l]), page_table, PAGES_PER_STEP)
        o_att = o8[:, :n_new].transpose(1, 0, 2).reshape(n_new * n_dec, width).astype(BF16)
        xl_tm = xl.reshape(n_new, n_dec, lru_width)
        conv_tm = state_conv[l].transpose(1, 0, 2)
        y_lru, h_s = _lru_sample(xl_tm, conv_tm, state_lru[l], *lru_w)
        x1, h2 = _merge(xs, o_att, y_lru.reshape(n_new * n_dec, lru_width), ga, gb,
                        wa_bf, wb_bf, wo_bf, row(g_ffn_norm[l]), n_new * n_dec)
        xs = _ffn(x1, h2, wg_bf, wu_bf, wd_bf, g_next, last, n_new * n_dec, FFN_TILE)
        hist_tm = jnp.concatenate([conv_tm, xl_tm], axis=0)
        outs[4].append(k_bm.reshape(n_dec, n_new, N_HEADS, HEAD_DIM))
        outs[5].append(v_bm.reshape(n_dec, n_new, N_HEADS, HEAD_DIM))
        outs[6].append(hist_tm[n_new:n_new + CONV_W - 1].transpose(1, 0, 2))
        outs[7].append(h_s)

    y_prompt = xp.reshape(n_prompt, seq, d)
    y_sample = xs.reshape(n_new, n_dec, d).transpose(1, 0, 2)
    return (y_prompt, y_sample) + tuple(jnp.stack(o) for o in outs)
```

```python
import functools

import jax
import jax.numpy as jnp
from jax import lax
from jax.experimental import pallas as pl
from jax.experimental.pallas import tpu as pltpu

F32 = jnp.float32
BF16 = jnp.bfloat16

LANES = 128
SUBLANES = 8
VMEM_LIMIT_BYTES = 56 * 1024 * 1024

HEAD_DIM = 64
N_HEADS = 16
LRU_BLOCK = 128
CONV_W = 4
LRU_C = 8.0
EPS = 1e-6
LOG2E = 1.4426950408889634
LN2 = 0.6931471805599453

NT_DIMS = (((1,), (1,)), ((), ()))
TN_DIMS = (((0,), (0,)), ((), ()))


def _params(*sem):
    return pltpu.CompilerParams(dimension_semantics=sem, vmem_limit_bytes=VMEM_LIMIT_BYTES)


def _rmsnorm(x, g):
    ms = jnp.mean(x * x, axis=-1, keepdims=True)
    return x * lax.rsqrt(ms + EPS) * g


EXP2_ARG_MAX = 126.0


def _sp2(z):
    return jnp.maximum(z, jnp.log(1.0 + jnp.exp2(jnp.minimum(z, EXP2_ARG_MAX))) * LOG2E)


def _in_proj_body(x_ref, g_ref, w_ref, k_ref, v_ref, qs_ref, kb_ref, vb_ref, xl_ref,
                  ga_ref, gb_ref, h_scr, *, kv_transposed):
    j = pl.program_id(1)
    kv_out = (lambda y: y.T) if kv_transposed else (lambda y: y)

    @pl.when(j == 0)
    def _():
        h_scr[...] = _rmsnorm(x_ref[...], g_ref[...]).astype(BF16)

    y = jnp.dot(h_scr[...], w_ref[...], preferred_element_type=F32)

    @pl.when(j == 0)
    def _():
        qs_ref[...] = (y * (HEAD_DIM ** -0.5 * LOG2E)).astype(BF16)

    @pl.when(j == 1)
    def _():
        k_ref[...] = kv_out(y)
        kb_ref[...] = y.astype(BF16)

    @pl.when(j == 2)
    def _():
        v_ref[...] = kv_out(y)
        vb_ref[...] = y.astype(BF16)

    @pl.when(j == 3)
    def _():
        xl_ref[...] = y

    @pl.when(j == 4)
    def _():
        ga_ref[...] = jax.nn.sigmoid(y)

    @pl.when(j == 5)
    def _():
        gb_ref[...] = jax.nn.sigmoid(y)


def _in_proj(x, g, w_bf, tm, kv_seq=None):
    m, d = x.shape
    n_groups = w_bf.shape[1] // d
    out_spec = pl.BlockSpec((tm, d), lambda i, j: (i, 0))
    f32o = jax.ShapeDtypeStruct((m, d), F32)
    bf16o = jax.ShapeDtypeStruct((m, d), BF16)
    kv_spec, kvo = out_spec, f32o
    if kv_seq is not None:
        steps = kv_seq // tm
        kv_spec = pl.BlockSpec((None, d, tm), lambda i, j: (i // steps, 0, i % steps))
        kvo = jax.ShapeDtypeStruct((m // kv_seq, d, kv_seq), F32)
    return pl.pallas_call(
        functools.partial(_in_proj_body, kv_transposed=kv_seq is not None),
        grid=(m // tm, n_groups),
        in_specs=[pl.BlockSpec((tm, d), lambda i, j: (i, 0)),
                  pl.BlockSpec((1, d), lambda i, j: (0, 0)),
                  pl.BlockSpec((d, d), lambda i, j: (0, j))],
        out_specs=[kv_spec, kv_spec] + [out_spec] * 6,
        out_shape=[kvo, kvo, bf16o, bf16o, bf16o, f32o, f32o, f32o],
        scratch_shapes=[pltpu.VMEM((tm, d), BF16)],
        compiler_params=_params("arbitrary", "arbitrary"),
        name="in_proj",
    )(x, g, w_bf)


ATTN_STAGES = 3
MASKED_LOGIT = -1e30


def _attn_prompt_body(it_ref, jt_ref, b_ref, q_ref, k_ref, v_ref, tri_ref, o_ref,
                      acc_ref, c_ref, z_ref, a_ref, *, tq, tk, n_items):
    p = pl.program_id(1)
    bias = (b_ref[2 * p], b_ref[2 * p + 1])
    lane = lax.broadcasted_iota(jnp.int32, (tq, LANES), 1)
    depth = ATTN_STAGES - 1
    n_slots = z_ref.shape[0]
    for ref in (acc_ref, c_ref, z_ref, a_ref):
        ref[...] = jnp.zeros_like(ref)

    def body(n, carry):
        i_out, j_out = it_ref[n], jt_ref[n]
        i_w, j_w = it_ref[n + 1], jt_ref[n + 1]
        i_new, j_new = it_ref[n + depth], jt_ref[n + depth]

        vj = v_ref[pl.ds(pl.multiple_of(j_out * tk, tk), tk), :]
        for hh in range(2):
            prev = jnp.where(j_out == i_out, 0.0, acc_ref[hh])
            acc_ref[hh] = prev + jnp.dot(a_ref[hh], vj, preferred_element_type=F32)

        zs = z_ref[(n + n_slots - 1) % n_slots]
        for hh in range(2):
            z = zs[hh]
            cum = jnp.dot(_sp2(z).astype(BF16), tri_ref[...], preferred_element_type=F32)
            c = jnp.where(j_w == i_w, 0.0, c_ref[hh])
            a = jnp.exp2(z + cum + jnp.concatenate([c] * (tk // LANES), axis=1))
            a_ref[hh] = a.astype(BF16)
            c_ref[hh] = c + jnp.broadcast_to(cum[:, :1], (tq, LANES))

        q = q_ref[pl.ds(pl.multiple_of(i_new * tq, tq), tq), :]
        kj = k_ref[pl.ds(pl.multiple_of(j_new * tk, tk), tk), :]
        zero = jnp.zeros_like(q)
        slot = n % n_slots
        q_heads = (jnp.where(lane < HEAD_DIM, q, zero), jnp.where(lane >= HEAD_DIM, q, zero))
        for hh in range(2):
            z = lax.dot_general(q_heads[hh], kj, NT_DIMS, preferred_element_type=F32)
            z_ref[slot, hh] = z + bias[hh]

        @pl.when(j_new == i_new)
        def _():
            row = lax.broadcasted_iota(jnp.int32, (tq, tk), 0)
            col = lax.broadcasted_iota(jnp.int32, (tq, tk), 1)
            for hh in range(2):
                z_ref[slot, hh] = jnp.where(col < row, z_ref[slot, hh], MASKED_LOGIT)

        @pl.when(j_out == 0)
        def _():
            out = jnp.where(lane < HEAD_DIM, acc_ref[0], acc_ref[1])
            o_ref[pl.ds(pl.multiple_of(i_out * tq, tq), tq), :] = out.astype(o_ref.dtype)

        return carry

    lax.fori_loop(0, n_items + depth, body, 0)


def _attn_prompt(qs, kb, vb, bias2, n_seq, seq, tq):
    tk = tq
    width = qs.shape[1]
    n_pairs = width // LANES
    nq = seq // tq
    depth = ATTN_STAGES - 1
    items = [(i, j) for i in range(nq) for j in range(i, -1, -1)]
    padded = [(0, 0)] * depth + items + [(nq - 1, 0)] * depth
    it = jnp.asarray([i for i, _ in padded], jnp.int32)
    jt = jnp.asarray([j for _, j in padded], jnp.int32)
    r = lax.broadcasted_iota(jnp.int32, (tk, tk), 0)
    c = lax.broadcasted_iota(jnp.int32, (tk, tk), 1)
    tri = jnp.where(r >= c, -1.0, 0.0).astype(BF16)
    seq_spec = pl.BlockSpec((seq, LANES), lambda b, p, it, jt: (b, p))
    grid_spec = pltpu.PrefetchScalarGridSpec(
        num_scalar_prefetch=2,
        grid=(n_seq, n_pairs),
        in_specs=[pl.BlockSpec(memory_space=pltpu.SMEM), seq_spec, seq_spec, seq_spec,
                  pl.BlockSpec((tk, tk), lambda b, p, it, jt: (0, 0))],
        out_specs=seq_spec,
        scratch_shapes=[pltpu.VMEM((2, tq, LANES), F32), pltpu.VMEM((2, tq, LANES), F32),
                        pltpu.VMEM((depth, 2, tq, tk), F32), pltpu.VMEM((2, tq, tk), BF16)],
    )
    return pl.pallas_call(
        functools.partial(_attn_prompt_body, tq=tq, tk=tk, n_items=len(items)),
        grid_spec=grid_spec,
        out_shape=jax.ShapeDtypeStruct(qs.shape, BF16),
        compiler_params=_params("arbitrary", "arbitrary"),
        name="attn_prompt",
    )(it, jt, bias2, qs, kb, vb, tri)


def _attn_sample_body(pt_ref, q_ref, kn_ref, vn_ref, bias_ref, tri_ref, sel_ref, *rest,
                      n_pages_step, n_new):
    del pt_ref
    kt_refs = rest[:n_pages_step]
    vt_refs = rest[n_pages_step:2 * n_pages_step]
    o_ref, qbd_ref, acc_ref, c_ref, pad_ref = rest[2 * n_pages_step:]
    jj = pl.program_id(1)
    n_cols, width = qbd_ref.shape
    page = tri_ref.shape[0]

    def keys_block(kt_list, vt_list, c, mask):
        n = len(kt_list)
        cat = lambda xs: xs[0] if len(xs) == 1 else jnp.concatenate(xs, axis=1)
        z = jnp.dot(qbd_ref[...], cat(kt_list), preferred_element_type=F32) + cat([bias_ref[...]] * n)
        sp = _sp2(z)
        if mask is not None:
            sp = jnp.where(mask, sp, 0.0)
        a_chunks = [None] * n
        for ch in reversed(range(n)):
            sl = slice(ch * page, (ch + 1) * page)
            cum = jnp.dot(sp[:, sl].astype(BF16), tri_ref[...], preferred_element_type=F32)
            a = jnp.exp2(z[:, sl] + cum + c)
            if mask is not None:
                a = jnp.where(mask, a, 0.0)
            a_chunks[ch] = a
            c = c + jnp.broadcast_to(cum[:, :1], c.shape)
        return c, lax.dot_general(cat(vt_list), cat(a_chunks), NT_DIMS, preferred_element_type=F32)

    @pl.when(jj == 0)
    def _():
        r = lax.broadcasted_iota(jnp.int32, (n_cols, width), 0)
        l = lax.broadcasted_iota(jnp.int32, (n_cols, width), 1)
        on_head = (l // HEAD_DIM == r // n_new) & (r < N_HEADS * n_new)
        q = q_ref[...].astype(F32)
        qbd = jnp.zeros((n_cols, width), F32)
        for t in range(n_new):
            qbd = jnp.where(on_head & (r % n_new == t), q[t:t + 1, :], qbd)
        qbd_ref[...] = qbd
        pad_ref[...] = jnp.zeros_like(pad_ref)
        pad_ref[0:n_new, :] = kn_ref[...]
        knt = pad_ref[...].T
        pad_ref[0:n_new, :] = vn_ref[...]
        vnt = pad_ref[...].T
        col = lax.broadcasted_iota(jnp.int32, (n_cols, page), 0)
        key = lax.broadcasted_iota(jnp.int32, (n_cols, page), 1)
        c_ref[...], acc_ref[...] = keys_block([knt], [vnt], jnp.zeros(c_ref.shape, F32),
                                              key < col % n_new)

    c = c_ref[...]
    out = None
    for g in reversed(range(n_pages_step // SAMPLE_PAGE_GROUP)):
        refs = slice(g * SAMPLE_PAGE_GROUP, (g + 1) * SAMPLE_PAGE_GROUP)
        c, part = keys_block([ref[...] for ref in kt_refs[refs]], [ref[...] for ref in vt_refs[refs]],
                             c, None)
        out = part if out is None else out + part
    c_ref[...] = c
    acc_ref[...] += out

    @pl.when(jj == pl.num_programs(1) - 1)
    def _():
        ch = lax.broadcasted_iota(jnp.int32, (width, n_cols), 0)
        col = lax.broadcasted_iota(jnp.int32, (width, n_cols), 1)
        own = jnp.where(ch // HEAD_DIM == col // n_new, acc_ref[...], 0.0)
        o_ref[...] = lax.dot_general(sel_ref[...], own.astype(BF16), NT_DIMS,
                                     preferred_element_type=F32)


def _attn_sample(q_bm, kn_bm, vn_bm, bias_rep, cache_kt, cache_vt, page_table, n_pages_step):
    n_dec, n_new, width = q_bm.shape
    n_pages = page_table.shape[1]
    page = cache_kt.shape[2]
    n_steps = n_pages // n_pages_step
    n_cols = LANES
    rr = lax.broadcasted_iota(jnp.int32, (page, page), 0)
    cc = lax.broadcasted_iota(jnp.int32, (page, page), 1)
    tri = jnp.where(rr >= cc, -1.0, 0.0).astype(BF16)
    sr = lax.broadcasted_iota(jnp.int32, (SUBLANES, n_cols), 0)
    sc = lax.broadcasted_iota(jnp.int32, (SUBLANES, n_cols), 1)
    sel = jnp.where((sc % n_new == sr) & (sc < N_HEADS * n_new) & (sr < n_new), 1.0, 0.0).astype(BF16)

    def page_map(pg):
        def index_map(b, jj, pt):
            logical = (n_steps - 1 - jj) * n_pages_step + pg
            return (pt[b * n_pages + logical], 0, 0)
        return index_map

    const = lambda shape: pl.BlockSpec(shape, lambda b, jj, pt: (0,) * len(shape))
    page_spec = lambda index_map: pl.BlockSpec((None, width, page), index_map)
    tok_spec = pl.BlockSpec((None, n_new, width), lambda b, jj, pt: (b, 0, 0))
    page_specs = [page_spec(page_map(pg)) for pg in range(n_pages_step)]
    grid_spec = pltpu.PrefetchScalarGridSpec(
        num_scalar_prefetch=1,
        grid=(n_dec, n_steps),
        in_specs=[tok_spec, tok_spec, tok_spec,
                  const((n_cols, page)), const((page, page)), const((SUBLANES, n_cols))]
                 + page_specs + page_specs,
        out_specs=pl.BlockSpec((None, SUBLANES, width), lambda b, jj, pt: (b, 0, 0)),
        scratch_shapes=[pltpu.VMEM((n_cols, width), F32),
                        pltpu.VMEM((width, n_cols), F32),
                        pltpu.VMEM((n_cols, page), F32),
                        pltpu.VMEM((page, width), F32)],
    )
    return pl.pallas_call(
        functools.partial(_attn_sample_body, n_pages_step=n_pages_step, n_new=n_new),
        grid_spec=grid_spec,
        out_shape=jax.ShapeDtypeStruct((n_dec, SUBLANES, width), F32),
        compiler_params=_params("arbitrary", "arbitrary"),
        name="attn_sample",
    )(page_table.reshape(-1), q_bm, kn_bm, vn_bm, bias_rep, tri, sel,
      *([cache_kt] * n_pages_step), *([cache_vt] * n_pages_step))


def _lru_gates(xc, wr_ref, br_ref, wi_ref, bi_ref, lam_ref):
    lam = lam_ref[...]
    neg_c_softplus = -LRU_C * (jnp.maximum(-lam, 0.0) + jnp.log1p(jnp.exp(-jnp.abs(lam))))
    xcb = xc.astype(BF16)
    a_parts, u_parts = [], []
    for n in range(xc.shape[1] // LRU_BLOCK):
        sl = slice(n * LRU_BLOCK, (n + 1) * LRU_BLOCK)
        r = jax.nn.sigmoid(jnp.dot(xcb[:, sl], wr_ref[n], preferred_element_type=F32) + br_ref[:, sl])
        ig = jax.nn.sigmoid(jnp.dot(xcb[:, sl], wi_ref[n], preferred_element_type=F32) + bi_ref[:, sl])
        a = jnp.exp(r * neg_c_softplus[:, sl])
        a_parts.append(a)
        u_parts.append(jnp.sqrt(1.0 - a * a) * (ig * xc[:, sl]))
    return a_parts, u_parts


def _lru_prompt_body(xl_ref, wc_ref, bc_ref, wr_ref, br_ref, wi_ref, bi_ref, lam_ref,
                     y_ref, hfin_ref, xbuf, h_scr, a_scr, u_scr, *, tb):
    t = pl.program_id(1)
    width = xl_ref.shape[1]
    pad = SUBLANES

    @pl.when(t == 0)
    def _():
        xbuf[0:pad, :] = jnp.zeros((pad, width), F32)
        h_scr[...] = jnp.zeros_like(h_scr)

    x = xl_ref[...]
    xbuf[pad:pad + tb, :] = x
    xc = bc_ref[...]
    for tap in range(CONV_W - 1):
        xc = xc + xbuf[pl.ds(pad - (CONV_W - 1) + tap, tb), :] * wc_ref[tap:tap + 1, :]
    xc = xc + x * wc_ref[CONV_W - 1:CONV_W, :]
    xbuf[0:pad, :] = x[tb - pad:, :]

    a_parts, u_parts = _lru_gates(xc, wr_ref, br_ref, wi_ref, bi_ref, lam_ref)
    for n, (a, u) in enumerate(zip(a_parts, u_parts)):
        sl = slice(n * LRU_BLOCK, (n + 1) * LRU_BLOCK)
        a_scr[:, sl] = a
        u_scr[:, sl] = u

    row = lax.broadcasted_iota(jnp.int32, (SUBLANES, width), 0)

    def group(g, h):
        r0 = pl.multiple_of(g * SUBLANES, SUBLANES)
        a = a_scr[pl.ds(r0, SUBLANES), :]
        u = u_scr[pl.ds(r0, SUBLANES), :]
        for d in (1, 2, 4):
            keep = row >= d
            a_prev = jnp.where(keep, pltpu.roll(a, d, 0), 1.0)
            u_prev = jnp.where(keep, pltpu.roll(u, d, 0), 0.0)
            u = u + a * u_prev
            a = a * a_prev
        hs = a * h + u
        u_scr[pl.ds(r0, SUBLANES), :] = hs
        return jnp.broadcast_to(hs[SUBLANES - 1:SUBLANES, :], (SUBLANES, width))

    h = lax.fori_loop(0, tb // SUBLANES, group, h_scr[...])
    h_scr[...] = h
    y_ref[...] = u_scr[...].astype(y_ref.dtype)
    hfin_ref[...] = h[:1, :]


def _lru_prompt(xl, wc, bc, wr_bf, br, wi_bf, bi, lam, n_seq, seq, tb):
    width = xl.shape[1]
    nt = seq // tb
    row_spec = pl.BlockSpec((1, width), lambda b, t: (0, 0))
    w_spec = pl.BlockSpec(wr_bf.shape, lambda b, t: (0, 0, 0))
    return pl.pallas_call(
        functools.partial(_lru_prompt_body, tb=tb),
        grid=(n_seq, nt),
        in_specs=[pl.BlockSpec((tb, width), lambda b, t: (b * nt + t, 0)),
                  pl.BlockSpec((CONV_W, width), lambda b, t: (0, 0)),
                  row_spec, w_spec, row_spec, w_spec, row_spec, row_spec],
        out_specs=[pl.BlockSpec((tb, width), lambda b, t: (b * nt + t, 0)),
                   pl.BlockSpec((None, 1, width), lambda b, t: (b, 0, 0))],
        out_shape=[jax.ShapeDtypeStruct(xl.shape, BF16),
                   jax.ShapeDtypeStruct((n_seq, 1, width), F32)],
        scratch_shapes=[pltpu.VMEM((tb + SUBLANES, width), F32),
                        pltpu.VMEM((SUBLANES, width), F32),
                        pltpu.VMEM((tb, width), F32),
                        pltpu.VMEM((tb, width), F32)],
        compiler_params=_params("arbitrary", "arbitrary"),
        name="lru_prompt",
    )(xl, wc, bc, wr_bf, br, wi_bf, bi, lam)


def _lru_sample_body(xl_ref, c0_ref, h0_ref, wc_ref, bc_ref, wr_ref, br_ref, wi_ref, bi_ref,
                     lam_ref, y_ref, hfin_ref):
    n_new = xl_ref.shape[0]
    hist = [c0_ref[i] for i in range(CONV_W - 1)] + [xl_ref[i] for i in range(n_new)]
    h = h0_ref[...]
    for t in range(n_new):
        xc = bc_ref[...]
        for tap in range(CONV_W):
            xc = xc + hist[tap + t] * wc_ref[tap:tap + 1, :]
        a_parts, u_parts = _lru_gates(xc, wr_ref, br_ref, wi_ref, bi_ref, lam_ref)
        h = jnp.concatenate(a_parts, axis=1) * h + jnp.concatenate(u_parts, axis=1)
        y_ref[t] = h.astype(y_ref.dtype)
    hfin_ref[...] = h


def _lru_sample(xl_tm, conv_tm, h0, wc, bc, wr_bf, br, wi_bf, bi, lam):
    n_new, n_dec, width = xl_tm.shape
    return pl.pallas_call(
        _lru_sample_body,
        out_shape=[jax.ShapeDtypeStruct((n_new, n_dec, width), BF16),
                   jax.ShapeDtypeStruct((n_dec, width), F32)],
        compiler_params=pltpu.CompilerParams(vmem_limit_bytes=VMEM_LIMIT_BYTES),
        name="lru_sample",
    )(xl_tm, conv_tm, h0, wc, bc, wr_bf, br, wi_bf, bi, lam)


def _merge_body(x_ref, oa_ref, yl_ref, ga_ref, gb_ref, wa_ref, wb_ref, wo_ref, g_ref,
                x1_ref, h2_ref):
    pa = jnp.dot(oa_ref[...], wa_ref[...], preferred_element_type=F32)
    pb = jnp.dot(yl_ref[...], wb_ref[...], preferred_element_type=F32)
    mixed = ga_ref[...] * pa + gb_ref[...] * pb
    x1 = x_ref[...] + jnp.dot(mixed.astype(BF16), wo_ref[...], preferred_element_type=F32)
    x1_ref[...] = x1
    h2_ref[...] = _rmsnorm(x1, g_ref[...]).astype(BF16)


def _merge(x, o_att, y_lru, ga, gb, wa_bf, wb_bf, wo_bf, g_ffn, tm):
    m, d = x.shape
    tile = pl.BlockSpec((tm, d), lambda i: (i, 0))
    w_spec = pl.BlockSpec((d, d), lambda i: (0, 0))
    return pl.pallas_call(
        _merge_body,
        grid=(m // tm,),
        in_specs=[tile, tile, tile, tile, tile, w_spec, w_spec, w_spec,
                  pl.BlockSpec((1, d), lambda i: (0, 0))],
        out_specs=[tile, tile],
        out_shape=[jax.ShapeDtypeStruct((m, d), F32), jax.ShapeDtypeStruct((m, d), BF16)],
        compiler_params=_params("arbitrary"),
        name="merge",
    )(x, o_att, y_lru, ga, gb, wa_bf, wb_bf, wo_bf, g_ffn)


def _ffn_body(x1_ref, h2_ref, wg_ref, wu_ref, wd_ref, g_ref, y_ref, acc_ref, *, final_norm):
    f = pl.program_id(1)

    @pl.when(f == 0)
    def _():
        acc_ref[...] = jnp.zeros_like(acc_ref)

    h2 = h2_ref[...]
    gate = jnp.dot(h2, wg_ref[...], preferred_element_type=F32)
    up = jnp.dot(h2, wu_ref[...], preferred_element_type=F32)
    act = (jax.nn.silu(gate) * up).astype(BF16)
    acc_ref[...] += jnp.dot(act, wd_ref[...], preferred_element_type=F32)

    @pl.when(f == pl.num_programs(1) - 1)
    def _():
        x2 = x1_ref[...] + acc_ref[...]
        y_ref[...] = _rmsnorm(x2, g_ref[...]) if final_norm else x2


def _ffn(x1, h2, wg_bf, wu_bf, wd_bf, g_final, final_norm, tm, tf):
    m, d = x1.shape
    d_ff = wg_bf.shape[1]
    tile = pl.BlockSpec((tm, d), lambda i, f: (i, 0))
    return pl.pallas_call(
        functools.partial(_ffn_body, final_norm=final_norm),
        grid=(m // tm, d_ff // tf),
        in_specs=[tile, tile,
                  pl.BlockSpec((d, tf), lambda i, f: (0, f)),
                  pl.BlockSpec((d, tf), lambda i, f: (0, f)),
                  pl.BlockSpec((tf, d), lambda i, f: (f, 0)),
                  pl.BlockSpec((1, d), lambda i, f: (0, 0))],
        out_specs=tile,
        out_shape=jax.ShapeDtypeStruct((m, d), F32),
        scratch_shapes=[pltpu.VMEM((tm, d), F32)],
        compiler_params=_params("arbitrary", "arbitrary"),
        name="ffn",
    )(x1, h2, wg_bf, wu_bf, wd_bf, g_final)


TOKEN_TILE = 512
FFN_TILE = 1408
ATTN_TILE = 256
LRU_TIME_TILE = 512
PAGES_PER_STEP = 16
SAMPLE_PAGE_GROUP = 16


def kernel(x_prompt, x_sample, cache_k, cache_v, state_conv, state_lru, page_table, g_mix_norm, w_in, b_att, w_conv, b_conv, w_r, b_r, w_i, b_i, lam, w_branch_a, w_branch_b, w_o, g_ffn_norm, w_gate, w_up, w_down, g_final):
    n_prompt, seq, d = x_prompt.shape
    n_dec, n_new, _ = x_sample.shape
    depth = w_in.shape[0]
    n_pool, page = cache_k.shape[1], cache_k.shape[2]
    width = N_HEADS * HEAD_DIM
    lru_width = w_conv.shape[2]
    assert d == width == lru_width, "kernels assume one shared channel width"

    xp = x_prompt.reshape(n_prompt * seq, d)
    xs = x_sample.transpose(1, 0, 2).reshape(n_new * n_dec, d)
    row = lambda v: v.reshape(1, -1)
    outs = [[] for _ in range(8)]
    for l in range(depth):
        last = l == depth - 1
        w_in_bf = w_in[l].astype(BF16)
        wr_bf, wi_bf = w_r[l].astype(BF16), w_i[l].astype(BF16)
        wa_bf, wb_bf, wo_bf = (w_branch_a[l].astype(BF16), w_branch_b[l].astype(BF16),
                               w_o[l].astype(BF16))
        wg_bf, wu_bf, wd_bf = w_gate[l].astype(BF16), w_up[l].astype(BF16), w_down[l].astype(BF16)
        bias2 = b_att[l].astype(F32) * LOG2E
        lru_w = (w_conv[l], row(b_conv[l]), wr_bf, row(b_r[l]), wi_bf, row(b_i[l]), row(lam[l]))
        g_next = row(g_final)

        k_p, v_p, qs, kb, vb, xl, ga, gb = _in_proj(xp, row(g_mix_norm[l]), w_in_bf, TOKEN_TILE, seq)
        o_att = _attn_prompt(qs, kb, vb, bias2, n_prompt, seq, ATTN_TILE)
        y_lru, h_p = _lru_prompt(xl, *lru_w, n_prompt, seq, LRU_TIME_TILE)
        x1, h2 = _merge(xp, o_att, y_lru, ga, gb, wa_bf, wb_bf, wo_bf, row(g_ffn_norm[l]), TOKEN_TILE)
        xp = _ffn(x1, h2, wg_bf, wu_bf, wd_bf, g_next, last, TOKEN_TILE, FFN_TILE)
        kv_view = lambda t: t.reshape(n_prompt, N_HEADS, HEAD_DIM, seq).transpose(0, 3, 1, 2)
        outs[0].append(kv_view(k_p))
        outs[1].append(kv_view(v_p))
        outs[2].append(xl.reshape(n_prompt, seq, lru_width)[:, seq - (CONV_W - 1):])
        outs[3].append(h_p.reshape(n_prompt, lru_width))

        k_s, v_s, qs, _, _, xl, ga, gb = _in_proj(xs, row(g_mix_norm[l]), w_in_bf, n_new * n_dec)
        to_bm = lambda v: v.reshape(n_new, n_dec, -1).transpose(1, 0, 2)
        k_bm, v_bm = to_bm(k_s), to_bm(v_s)
        bias_cols = jnp.pad(jnp.repeat(bias2, n_new), (0, LANES - N_HEADS * n_new))
        bias_rep = jnp.broadcast_to(bias_cols[:, None], (LANES, page))
        to_pages_t = lambda c: c.transpose(0, 2, 3, 1).reshape(c.shape[0], width, page)
        o8 = _attn_sample(to_bm(qs), k_bm, v_bm, bias_rep,
                          to_pages_t(cache_k[l]), to_pages_t(cache_v[l]), page_table, PAGES_PER_STEP)
        o_att = o8[:, :n_new].transpose(1, 0, 2).reshape(n_new * n_dec, width).astype(BF16)
        xl_tm = xl.reshape(n_new, n_dec, lru_width)
        conv_tm = state_conv[l].transpose(1, 0, 2)
        y_lru, h_s = _lru_sample(xl_tm, conv_tm, state_lru[l], *lru_w)
        x1, h2 = _merge(xs, o_att, y_lru.reshape(n_new * n_dec, lru_width), ga, gb,
                        wa_bf, wb_bf, wo_bf, row(g_ffn_norm[l]), n_new * n_dec)
        xs = _ffn(x1, h2, wg_bf, wu_bf, wd_bf, g_next, last, n_new * n_dec, FFN_TILE)
        hist_tm = jnp.concatenate([conv_tm, xl_tm], axis=0)
        outs[4].append(k_bm.reshape(n_dec, n_new, N_HEADS, HEAD_DIM))
        outs[5].append(v_bm.reshape(n_dec, n_new, N_HEADS, HEAD_DIM))
        outs[6].append(hist_tm[n_new:n_new + CONV_W - 1].transpose(1, 0, 2))
        outs[7].append(h_s)

    y_prompt = xp.reshape(n_prompt, seq, d)
    y_sample = xs.reshape(n_new, n_dec, d).transpose(1, 0, 2)
    return (y_prompt, y_sample) + tuple(jnp.stack(o) for o in outs)
```
